```python
import jax, jax.numpy as jnp
from jax import lax
import numpy as np

D_MODEL = 4096
BATCH = 4
SEQ = 2048
DEPTH = 4
DEC_BATCH = 128
DEC_SEQ = 4
PAST_LEN = 16384
PAGE_SIZE = 128

N_MIXERS = 2
POOL_WINDOWS = (2, 4, 8, 16)
N_POOL_GROUPS = len(POOL_WINDOWS)
POOL_GROUP_DIM = D_MODEL // N_POOL_GROUPS
POOL_STATE_LEN = max(POOL_WINDOWS) - 1
CONV_WIDTH = 3
CONV_STATE_LEN = CONV_WIDTH - 1
D_FF = 4 * D_MODEL
N_POOL_LAYERS = (DEPTH + 1) // 2
N_CONV_LAYERS = DEPTH // 2
RMS_EPS = 1e-6

kernel_name = "interleaved_pool_shortconv_decoder_step"


def rmsnorm(x, g):
    xf = x.astype(jnp.float32)
    inv = lax.rsqrt(jnp.mean(xf * xf, axis=-1, keepdims=True) + RMS_EPS)
    return (xf * inv).astype(x.dtype) * g


def pool_mixer(u, prev, start, w_pool, pool_scale):
    b, t, d = u.shape
    p = POOL_STATE_LEN
    padded = jnp.concatenate([prev.astype(u.dtype), u], axis=1)
    cs = jnp.cumsum(padded.astype(jnp.float32), axis=1)
    cs = jnp.pad(cs, ((0, 0), (1, 0), (0, 0)))
    end = cs[:, p + 1:]
    pos = start + jnp.arange(t, dtype=jnp.int32)
    pooled = []
    for g, w in enumerate(POOL_WINDOWS):
        sl = slice(g * POOL_GROUP_DIM, (g + 1) * POOL_GROUP_DIM)
        s = end[..., sl] - cs[:, p + 1 - w: p + 1 - w + t, sl]
        cnt = jnp.minimum(pos + 1, w).astype(jnp.float32)[None, :, None]
        pooled.append(s / cnt)
    pooled = jnp.concatenate(pooled, axis=-1)
    diff = (pooled - u.astype(jnp.float32)).astype(u.dtype)
    diff = diff.reshape(b, t, N_POOL_GROUPS, POOL_GROUP_DIM)
    y = jnp.einsum("btgc,gcd->btgd", diff, w_pool).reshape(b, t, d) * pool_scale
    new_prev = padded[:, -p:]
    return y, new_prev


def conv_mixer(u, prev, w_in, conv_w, w_out):
    t = u.shape[1]
    z = jnp.einsum("btd,de->bte", u, w_in)
    gate_b, gate_c, h = jnp.split(z, 3, axis=-1)
    v = gate_c * h
    padded = jnp.concatenate([prev.astype(v.dtype), v], axis=1)
    conv = conv_w[0] * padded[:, 0:t]
    for k in range(1, CONV_WIDTH):
        conv = conv + conv_w[k] * padded[:, k:k + t]
    y = jnp.einsum("btd,de->bte", gate_b * conv, w_out)
    new_prev = padded[:, -CONV_STATE_LEN:]
    return y, new_prev


def trunk(x, start, pool_state, conv_state, norm_mix, norm_mlp, norm_final,
          w_pool, pool_scale, w_conv_in, conv_w, w_conv_out, w_up, w_down):
    new_pool, new_conv = [], []
    for i in range(DEPTH):
        u = rmsnorm(x, norm_mix[i])
        j = i // N_MIXERS
        if i % N_MIXERS == 0:
            m, st = pool_mixer(u, pool_state[j], start, w_pool[j], pool_scale[j])
            new_pool.append(st)
        else:
            m, st = conv_mixer(u, conv_state[j], w_conv_in[j], conv_w[j], w_conv_out[j])
            new_conv.append(st)
        x = x + m
        u = rmsnorm(x, norm_mlp[i])
        hdn = jnp.square(jax.nn.relu(jnp.einsum("btd,df->btf", u, w_up[i])))
        x = x + jnp.einsum("btf,fd->btd", hdn, w_down[i])
    y = rmsnorm(x, norm_final)
    return y, jnp.stack(new_pool), jnp.stack(new_conv)


def setup_inputs(seed: int = 0) -> dict:
    key = jax.random.key(seed)
    ks = jax.random.split(key, 16)
    f32 = jnp.float32
    nrm = jax.random.normal
    x_prompt = nrm(ks[0], (BATCH, SEQ, D_MODEL), f32)
    x_sample = nrm(ks[1], (DEC_BATCH, DEC_SEQ, D_MODEL), f32)
    state_pool = nrm(ks[2], (N_POOL_LAYERS, DEC_BATCH, POOL_STATE_LEN, D_MODEL), f32)
    state_conv = nrm(ks[3], (N_CONV_LAYERS, DEC_BATCH, CONV_STATE_LEN, D_MODEL), f32)
    norm_mix = 1.0 + 0.02 * nrm(ks[4], (DEPTH, D_MODEL), f32)
    norm_mlp = 1.0 + 0.02 * nrm(ks[5], (DEPTH, D_MODEL), f32)
    norm_final = 1.0 + 0.02 * nrm(ks[6], (D_MODEL,), f32)
    w_pool = nrm(ks[7], (N_POOL_LAYERS, N_POOL_GROUPS, POOL_GROUP_DIM, POOL_GROUP_DIM), f32) * POOL_GROUP_DIM ** -0.5
    pool_scale = 1.0 + 0.1 * nrm(ks[8], (N_POOL_LAYERS, D_MODEL), f32)
    w_conv_in = nrm(ks[9], (N_CONV_LAYERS, D_MODEL, 3 * D_MODEL), f32) * D_MODEL ** -0.5
    conv_w = nrm(ks[10], (N_CONV_LAYERS, CONV_WIDTH, D_MODEL), f32) * CONV_WIDTH ** -0.5
    w_conv_out = nrm(ks[11], (N_CONV_LAYERS, D_MODEL, D_MODEL), f32) * D_MODEL ** -0.5
    w_up = nrm(ks[12], (DEPTH, D_MODEL, D_FF), f32) * D_MODEL ** -0.5
    w_down = nrm(ks[13], (DEPTH, D_FF, D_MODEL), f32) * (0.5 * D_FF ** -0.5)
    return {"x_prompt": x_prompt, "x_sample": x_sample,
            "state_pool": state_pool, "state_conv": state_conv,
            "norm_mix": norm_mix, "norm_mlp": norm_mlp, "norm_final": norm_final,
            "w_pool": w_pool, "pool_scale": pool_scale,
            "w_conv_in": w_conv_in, "conv_w": conv_w, "w_conv_out": w_conv_out,
            "w_up": w_up, "w_down": w_down}


def reference(x_prompt, x_sample, state_pool, state_conv, norm_mix, norm_mlp, norm_final,
              w_pool, pool_scale, w_conv_in, conv_w, w_conv_out, w_up, w_down):
    b = x_prompt.shape[0]
    zero_pool = jnp.zeros((N_POOL_LAYERS, b, POOL_STATE_LEN, D_MODEL), x_prompt.dtype)
    zero_conv = jnp.zeros((N_CONV_LAYERS, b, CONV_STATE_LEN, D_MODEL), x_prompt.dtype)
    y_prompt, new_pool_prompt, new_conv_prompt = trunk(
        x_prompt, 0, zero_pool, zero_conv, norm_mix, norm_mlp, norm_final,
        w_pool, pool_scale, w_conv_in, conv_w, w_conv_out, w_up, w_down)
    y_sample, new_pool_sample, new_conv_sample = trunk(
        x_sample, PAST_LEN, state_pool, state_conv, norm_mix, norm_mlp, norm_final,
        w_pool, pool_scale, w_conv_in, conv_w, w_conv_out, w_up, w_down)
    return (y_prompt, y_sample, new_pool_prompt, new_pool_sample, new_conv_prompt, new_conv_sample)
```

```python
import functools

import jax
import jax.numpy as jnp
from jax import lax
from jax.experimental import pallas as pl
from jax.experimental.pallas import tpu as pltpu

DEPTH = 4
PAST_LEN = 16384
POOL_WINDOWS = (2, 4, 8, 16)
N_GROUPS = len(POOL_WINDOWS)
POOL_STATE_LEN = max(POOL_WINDOWS) - 1
CONV_WIDTH = 3
CONV_STATE_LEN = CONV_WIDTH - 1
RMS_EPS = 1e-6

F32 = jnp.float32
BF16 = jnp.bfloat16

V7X_VMEM_BYTES = 64 * 1024 * 1024
VMEM_LIMIT = V7X_VMEM_BYTES - 6 * 1024 * 1024
F32_SUBLANES = 8
BF16_SUBLANES = 16
HALO = 2 * F32_SUBLANES

TILES = dict(
    ffn_tm=512, ffn_tf=512, ffn_tn=512,
    pool_tt=256, pool_bb=16,
    conv_tm=512, conv_tn=512,
    proj_tm=1024, proj_tn=1024,
)


def _params(sem):
    return pltpu.CompilerParams(dimension_semantics=sem, vmem_limit_bytes=VMEM_LIMIT)


def _rms(x, g):
    inv = lax.rsqrt(jnp.mean(x * x, axis=-1, keepdims=True) + RMS_EPS)
    return (x * inv) * g


ROW_CHUNK = 2 * BF16_SUBLANES


def _for_row_chunks(n_rows, fn):
    rc = min(ROW_CHUNK, n_rows)

    def body(r, carry):
        fn(pl.ds(pl.multiple_of(r * rc, rc), rc))
        return carry

    lax.fori_loop(0, n_rows // rc, body, 0)


def _ffn_kernel(*refs, n_cast, tn, final):
    x_ref, g_ref, gf_ref, wup_ref, wdn_ref = refs[:5]
    cast_in = refs[5:5 + n_cast]
    o_ref = refs[5 + n_cast]
    cast_out = refs[6 + n_cast:6 + 2 * n_cast]
    u_ref = refs[6 + 2 * n_cast]
    j = pl.program_id(1)

    @pl.when(j == 0)
    def _():
        def norm_rows(rows):
            x = x_ref[rows, :]
            u_ref[rows, :] = _rms(x, g_ref[...]).astype(BF16)
            o_ref[rows, :] = x

        _for_row_chunks(x_ref.shape[0], norm_rows)

    h = jnp.dot(u_ref[...], wup_ref[...], preferred_element_type=F32)
    h = jnp.maximum(h, 0.0)
    h = (h * h).astype(BF16)
    for n in range(0, o_ref.shape[1], tn):
        o_ref[:, n:n + tn] += jnp.dot(h, wdn_ref[:, n:n + tn], preferred_element_type=F32)

    for src, dst in zip(cast_in, cast_out):
        dst[...] = src[...].astype(BF16)

    if final:
        @pl.when(j == pl.num_programs(1) - 1)
        def _():
            def final_rows(rows):
                o_ref[rows, :] = _rms(o_ref[rows, :], gf_ref[...])

            _for_row_chunks(o_ref.shape[0], final_rows)


def _cast_specs(w, layer, n_steps, nj):
    _, r, c = w.shape
    reps = 1
    while (r * reps) % n_steps or (r * reps // n_steps) % BF16_SUBLANES:
        reps *= 2
        assert reps <= n_steps, (w.shape, n_steps)
    rb = r * reps // n_steps
    in_spec = pl.BlockSpec((None, rb, c), lambda i, j: (layer, (i * nj + j) // reps, 0))
    out_spec = pl.BlockSpec((rb, c), lambda i, j: ((i * nj + j) // reps, 0))
    return in_spec, out_spec, jax.ShapeDtypeStruct((r, c), BF16)


def _ffn(x, g, gf, wup, wdn, cast_jobs=(), *, final):
    t, d = x.shape
    f = wup.shape[1]
    tm, tf, tn = min(TILES["ffn_tm"], t), min(TILES["ffn_tf"], f), min(TILES["ffn_tn"], d)
    ni, nj = t // tm, f // tf
    cast = [_cast_specs(w, layer, ni * nj, nj) for w, layer in cast_jobs]
    outs = pl.pallas_call(
        functools.partial(_ffn_kernel, n_cast=len(cast), tn=tn, final=final),
        grid=(ni, nj),
        in_specs=[
            pl.BlockSpec((tm, d), lambda i, j: (i, 0), pipeline_mode=pl.Buffered(1)),
            pl.BlockSpec((1, d), lambda i, j: (0, 0)),
            pl.BlockSpec((1, d), lambda i, j: (0, 0)),
            pl.BlockSpec((d, tf), lambda i, j: (0, j)),
            pl.BlockSpec((tf, d), lambda i, j: (j, 0)),
        ] + [c[0] for c in cast],
        out_specs=[pl.BlockSpec((tm, d), lambda i, j: (i, 0))] + [c[1] for c in cast],
        out_shape=[jax.ShapeDtypeStruct((t, d), F32)] + [c[2] for c in cast],
        scratch_shapes=[pltpu.VMEM((tm, d), BF16)],
        compiler_params=_params(("arbitrary", "arbitrary")),
        name="ffn",
    )(x, g, gf, wup, wdn, *[w for w, _ in cast_jobs])
    return outs[0], list(outs[1:])


def _pool_prompt_kernel(x_ref, xh_ref, g_ref, w_ref, sc_ref, o_ref, st_ref, p_ref, *, tt):
    t = pl.program_id(1)
    g = g_ref[...]
    d = x_ref.shape[2]
    gd = d // N_GROUPS
    def norm_rows(rows):
        dst = pl.ds(pl.multiple_of(HALO + rows.start, HALO), rows.size)
        p_ref[dst, :] = _rms(x_ref[0, rows, :], g)

    _for_row_chunks(tt, norm_rows)

    @pl.when(t == 0)
    def _():
        p_ref[0:HALO, :] = jnp.zeros((HALO, d), F32)

    @pl.when(t > 0)
    def _():
        p_ref[0:HALO, :] = _rms(xh_ref[0], g)

    pos = t * tt + lax.broadcasted_iota(jnp.int32, (tt, 1), 0)
    for gi, w in enumerate(POOL_WINDOWS):
        sl = slice(gi * gd, (gi + 1) * gd)
        u = p_ref[HALO:HALO + tt, sl]
        s = u
        for k in range(1, w):
            s = s + p_ref[HALO - k:HALO - k + tt, sl]
        cnt = jnp.minimum(pos + 1, w).astype(F32)
        diff = (s / cnt - u).astype(BF16)
        y = jnp.dot(diff, w_ref[gi], preferred_element_type=F32) * sc_ref[:, sl]
        o_ref[0, :, sl] = x_ref[0, :, sl] + y
    st_ref[0] = p_ref[HALO + tt - POOL_STATE_LEN:HALO + tt, :]


def _pool_prompt(x, g, w, layer, sc):
    b, s, d = x.shape
    gd = d // N_GROUPS
    tt = min(TILES["pool_tt"], s)
    hb = tt // HALO
    return pl.pallas_call(
        functools.partial(_pool_prompt_kernel, tt=tt),
        grid=(b, s // tt),
        in_specs=[
            pl.BlockSpec((1, tt, d), lambda i, t: (i, t, 0)),
            pl.BlockSpec((1, HALO, d), lambda i, t: (i, jnp.maximum(t * hb - 1, 0), 0)),
            pl.BlockSpec((1, d), lambda i, t: (0, 0)),
            pl.BlockSpec((None, N_GROUPS, gd, gd), lambda i, t: (layer, 0, 0, 0), pipeline_mode=pl.Buffered(1)),
            pl.BlockSpec((1, d), lambda i, t: (0, 0)),
        ],
        out_specs=[
            pl.BlockSpec((1, tt, d), lambda i, t: (i, t, 0)),
            pl.BlockSpec((1, POOL_STATE_LEN, d), lambda i, t: (i, 0, 0)),
        ],
        out_shape=[
            jax.ShapeDtypeStruct((b, s, d), F32),
            jax.ShapeDtypeStruct((b, POOL_STATE_LEN, d), F32),
        ],
        scratch_shapes=[pltpu.VMEM((HALO + tt, d), F32)],
        compiler_params=_params(("arbitrary", "arbitrary")),
        name="pool_prompt",
    )(x, x, g, w, sc)


def _pool_sample_kernel(x_ref, s_ref, g_ref, w_ref, sc_ref, o_ref, so_ref, u_ref, d_ref, *, bb, n_t):
    d = u_ref.shape[2]
    gd = d // N_GROUPS
    g = g_ref[...]
    for t in range(n_t):
        u_ref[t] = _rms(x_ref[:, t * d:(t + 1) * d], g)

    def padded_row(idx, sl):
        if idx < POOL_STATE_LEN:
            return s_ref[:, idx * d + sl.start:idx * d + sl.stop]
        return u_ref[idx - POOL_STATE_LEN, :, sl]

    for gi, w in enumerate(POOL_WINDOWS):
        sl = slice(gi * gd, (gi + 1) * gd)
        for t in range(n_t):
            s = padded_row(POOL_STATE_LEN + t, sl)
            for k in range(1, w):
                s = s + padded_row(POOL_STATE_LEN + t - k, sl)
            cnt = float(min(PAST_LEN + t + 1, w))
            d_ref[t * bb:(t + 1) * bb, sl] = (s / cnt - u_ref[t, :, sl]).astype(BF16)
        y = jnp.dot(d_ref[:, sl], w_ref[gi], preferred_element_type=F32) * sc_ref[:, sl]
        for t in range(n_t):
            cs = slice(t * d + sl.start, t * d + sl.stop)
            o_ref[:, cs] = x_ref[:, cs] + y[t * bb:(t + 1) * bb]

    keep = POOL_STATE_LEN - n_t
    so_ref[:, 0:keep * d] = s_ref[:, n_t * d:POOL_STATE_LEN * d]
    for t in range(n_t):
        so_ref[:, (keep + t) * d:(keep + t + 1) * d] = u_ref[t]


def _pool_sample(x, state, g, w, layer, sc):
    b, td = x.shape
    d = g.shape[1]
    gd = d // N_GROUPS
    n_t = td // d
    assert n_t <= POOL_STATE_LEN
    sd = state.shape[1]
    bb = min(TILES["pool_bb"], b)
    return pl.pallas_call(
        functools.partial(_pool_sample_kernel, bb=bb, n_t=n_t),
        grid=(b // bb,),
        in_specs=[
            pl.BlockSpec((bb, td), lambda i: (i, 0)),
            pl.BlockSpec((bb, sd), lambda i: (i, 0)),
            pl.BlockSpec((1, d), lambda i: (0, 0)),
            pl.BlockSpec((None, N_GROUPS, gd, gd), lambda i: (layer, 0, 0, 0), pipeline_mode=pl.Buffered(1)),
            pl.BlockSpec((1, d), lambda i: (0, 0)),
        ],
        out_specs=[
            pl.BlockSpec((bb, td), lambda i: (i, 0)),
            pl.BlockSpec((bb, sd), lambda i: (i, 0)),
        ],
        out_shape=[
            jax.ShapeDtypeStruct((b, td), F32),
            jax.ShapeDtypeStruct((b, sd), F32),
        ],
        scratch_shapes=[pltpu.VMEM((n_t, bb, d), F32), pltpu.VMEM((n_t * bb, d), BF16)],
        compiler_params=_params(("arbitrary",)),
        name="pool_sample",
    )(x, state, g, w, sc)


def _conv_in_prompt_kernel(x_ref, g_ref, wb_ref, wc_ref, wh_ref, cw_ref, o_ref, st_ref,
                           u_ref, v_ref, carry_ref, *, tm):
    t = pl.program_id(1)
    c = pl.program_id(2)
    sub = F32_SUBLANES

    @pl.when(c == 0)
    def _():
        def norm_rows(rows):
            u_ref[rows, :] = _rms(x_ref[0, rows, :], g_ref[...]).astype(BF16)

        _for_row_chunks(tm, norm_rows)

    u = u_ref[...]
    zc = jnp.dot(u, wc_ref[...], preferred_element_type=F32)
    zh = jnp.dot(u, wh_ref[...], preferred_element_type=F32)
    v_ref[sub:sub + tm, :] = zc * zh

    @pl.when(t == 0)
    def _():
        v_ref[0:sub, :] = jnp.zeros((sub, v_ref.shape[1]), F32)

    @pl.when(t > 0)
    def _():
        v_ref[0:sub, :] = carry_ref[c]

    conv = cw_ref[CONV_WIDTH - 1:CONV_WIDTH, :] * v_ref[sub:sub + tm, :]
    for k in range(CONV_WIDTH - 1):
        off = sub - (CONV_WIDTH - 1) + k
        conv = conv + cw_ref[k:k + 1, :] * v_ref[off:off + tm, :]
    zb = jnp.dot(u, wb_ref[...], preferred_element_type=F32)
    o_ref[0] = (zb * conv).astype(BF16)
    carry_ref[c] = v_ref[tm:tm + sub, :]

    @pl.when(t == pl.num_programs(1) - 1)
    def _():
        st_ref[0, c] = v_ref[sub + tm - CONV_STATE_LEN:sub + tm, :]


def _conv_in_prompt(x, g, w_in, layer, cw):
    b, s, d = x.shape
    tm, tn = min(TILES["conv_tm"], s), min(TILES["conv_tn"], d)
    nc = d // tn
    gated, st = pl.pallas_call(
        functools.partial(_conv_in_prompt_kernel, tm=tm),
        grid=(b, s // tm, nc),
        in_specs=[
            pl.BlockSpec((1, tm, d), lambda i, t, c: (i, t, 0), pipeline_mode=pl.Buffered(1)),
            pl.BlockSpec((1, d), lambda i, t, c: (0, 0)),
            pl.BlockSpec((d, tn), lambda i, t, c: (0, c)),
            pl.BlockSpec((d, tn), lambda i, t, c: (0, nc + c)),
            pl.BlockSpec((d, tn), lambda i, t, c: (0, 2 * nc + c)),
            pl.BlockSpec((None, CONV_WIDTH, tn), lambda i, t, c: (layer, 0, c)),
        ],
        out_specs=[
            pl.BlockSpec((1, tm, tn), lambda i, t, c: (i, t, c)),
            pl.BlockSpec((1, nc, CONV_STATE_LEN, tn), lambda i, t, c: (i, 0, 0, 0)),
        ],
        out_shape=[
            jax.ShapeDtypeStruct((b, s, d), BF16),
            jax.ShapeDtypeStruct((b, nc, CONV_STATE_LEN, tn), F32),
        ],
        scratch_shapes=[
            pltpu.VMEM((tm, d), BF16),
            pltpu.VMEM((F32_SUBLANES + tm, tn), F32),
            pltpu.VMEM((nc, F32_SUBLANES, tn), F32),
        ],
        compiler_params=_params(("arbitrary", "arbitrary", "arbitrary")),
        name="conv_in_prompt",
    )(x, g, w_in, w_in, w_in, cw)
    return gated, st.transpose(0, 2, 1, 3).reshape(b, CONV_STATE_LEN, d)


def _conv_in_sample_kernel(x_ref, s0_ref, s1_ref, g_ref, wb_ref, wc_ref, wh_ref, cw_ref,
                           o_ref, st0_ref, st1_ref, u_ref, *, b, n_t):
    c = pl.program_id(0)
    d = u_ref.shape[1]

    @pl.when(c == 0)
    def _():
        for t in range(n_t):
            u_ref[t * b:(t + 1) * b, :] = _rms(x_ref[:, t * d:(t + 1) * d], g_ref[...]).astype(BF16)

    u = u_ref[...]
    zc = jnp.dot(u, wc_ref[...], preferred_element_type=F32)
    zh = jnp.dot(u, wh_ref[...], preferred_element_type=F32)
    zb = jnp.dot(u, wb_ref[...], preferred_element_type=F32)
    v = zc * zh
    rows = [s0_ref[...], s1_ref[...]] + [v[t * b:(t + 1) * b] for t in range(n_t)]
    for t in range(n_t):
        conv = cw_ref[0:1, :] * rows[t]
        for k in range(1, CONV_WIDTH):
            conv = conv + cw_ref[k:k + 1, :] * rows[t + k]
        o_ref[t * b:(t + 1) * b, :] = (zb[t * b:(t + 1) * b] * conv).astype(BF16)
    st0_ref[...] = rows[n_t]
    st1_ref[...] = rows[n_t + 1]


def _conv_in_sample(x, state, g, w_in, layer, cw):
    b, td = x.shape
    d = g.shape[1]
    n_t = td // d
    tn = min(TILES["conv_tn"], d)
    nc = d // tn
    assert CONV_STATE_LEN == 2
    return pl.pallas_call(
        functools.partial(_conv_in_sample_kernel, b=b, n_t=n_t),
        grid=(nc,),
        in_specs=[
            pl.BlockSpec((b, td), lambda c: (0, 0)),
            pl.BlockSpec((b, tn), lambda c: (0, c)),
            pl.BlockSpec((b, tn), lambda c: (0, nc + c)),
            pl.BlockSpec((1, d), lambda c: (0, 0)),
            pl.BlockSpec((d, tn), lambda c: (0, c)),
            pl.BlockSpec((d, tn), lambda c: (0, nc + c)),
            pl.BlockSpec((d, tn), lambda c: (0, 2 * nc + c)),
            pl.BlockSpec((None, CONV_WIDTH, tn), lambda c: (layer, 0, c)),
        ],
        out_specs=[
            pl.BlockSpec((n_t * b, tn), lambda c: (0, c)),
            pl.BlockSpec((b, tn), lambda c: (0, c)),
            pl.BlockSpec((b, tn), lambda c: (0, c)),
        ],
        out_shape=[
            jax.ShapeDtypeStruct((n_t * b, d), BF16),
            jax.ShapeDtypeStruct((b, d), F32),
            jax.ShapeDtypeStruct((b, d), F32),
        ],
        scratch_shapes=[pltpu.VMEM((n_t * b, d), BF16)],
        compiler_params=_params(("arbitrary",)),
        name="conv_in_sample",
    )(x, state, state, g, w_in, w_in, w_in, cw)


def _proj_res_kernel(a_ref, w_ref, x_ref, o_ref):
    o_ref[...] = x_ref[...] + jnp.dot(a_ref[...], w_ref[...], preferred_element_type=F32)


def _conv_out_prompt(a, w, x):
    t, d = x.shape
    tm, tn = min(TILES["proj_tm"], t), min(TILES["proj_tn"], d)
    return pl.pallas_call(
        _proj_res_kernel,
        grid=(t // tm, d // tn),
        in_specs=[
            pl.BlockSpec((tm, d), lambda i, n: (i, 0)),
            pl.BlockSpec((d, tn), lambda i, n: (0, n)),
            pl.BlockSpec((tm, tn), lambda i, n: (i, n)),
        ],
        out_specs=pl.BlockSpec((tm, tn), lambda i, n: (i, n)),
        out_shape=jax.ShapeDtypeStruct((t, d), F32),
        compiler_params=_params(("arbitrary", "arbitrary")),
        name="conv_out_prompt",
    )(a, w, x)


def _conv_out_sample(a, w, x):
    b, td = x.shape
    d = w.shape[0]
    n_t = td // d
    tn = min(TILES["proj_tn"], d)
    nn = d // tn
    return pl.pallas_call(
        _proj_res_kernel,
        grid=(n_t, nn),
        in_specs=[
            pl.BlockSpec((b, d), lambda t, n: (t, 0)),
            pl.BlockSpec((d, tn), lambda t, n: (0, n)),
            pl.BlockSpec((b, tn), lambda t, n: (0, t * nn + n)),
        ],
        out_specs=pl.BlockSpec((b, tn), lambda t, n: (0, t * nn + n)),
        out_shape=jax.ShapeDtypeStruct((b, td), F32),
        compiler_params=_params(("arbitrary", "arbitrary")),
        name="conv_out_sample",
    )(a, w, x)


def kernel(x_prompt, x_sample, state_pool, state_conv, norm_mix, norm_mlp, norm_final,
           w_pool, pool_scale, w_conv_in, conv_w, w_conv_out, w_up, w_down):
    bp, sp, d = x_prompt.shape
    bs, ss, _ = x_sample.shape

    w_pool_b = w_pool.astype(BF16)
    wup_b, wdn_b = w_up[0].astype(BF16), w_down[0].astype(BF16)
    wci_b = wco_b = None
    g_fin = norm_final.reshape(1, d)

    xp = x_prompt
    xs = x_sample.reshape(bs, ss * d)
    pool_p, pool_s, conv_p, conv_s = [], [], [], []
    for i in range(DEPTH):
        j = i // 2
        last = i == DEPTH - 1
        g_mix = norm_mix[i].reshape(1, d)
        if i % 2 == 0:
            sc = pool_scale[j].reshape(1, d)
            xp, st = _pool_prompt(xp, g_mix, w_pool_b, j, sc)
            pool_p.append(st)
            xs, st = _pool_sample(xs, state_pool[j].reshape(bs, POOL_STATE_LEN * d), g_mix, w_pool_b, j, sc)
            pool_s.append(st.reshape(bs, POOL_STATE_LEN, d))
        else:
            a, st = _conv_in_prompt(xp, g_mix, wci_b, j, conv_w)
            conv_p.append(st)
            xp = _conv_out_prompt(a.reshape(bp * sp, d), wco_b, xp.reshape(bp * sp, d)).reshape(bp, sp, d)
            a, st0, st1 = _conv_in_sample(xs, state_conv[j].reshape(bs, CONV_STATE_LEN * d), g_mix,
                                          wci_b, j, conv_w)
            conv_s.append(jnp.stack([st0, st1], axis=1))
            xs = _conv_out_sample(a, wco_b, xs)
        g_mlp = norm_mlp[i].reshape(1, d)
        jobs = []
        if not last:
            if (i + 1) % 2 == 1:
                jobs += [(w_conv_in, (i + 1) // 2), (w_conv_out, (i + 1) // 2)]
            jobs += [(w_up, i + 1), (w_down, i + 1)]
        xp, cast = _ffn(xp.reshape(bp * sp, d), g_mlp, g_fin, wup_b, wdn_b, jobs, final=last)
        xp = xp.reshape(bp, sp, d)
        xs, _ = _ffn(xs.reshape(bs * ss, d), g_mlp, g_fin, wup_b, wdn_b, final=last)
        xs = xs.reshape(bs, ss * d)
        if not last:
            if (i + 1) % 2 == 1:
                wci_b, wco_b = cast[0], cast[1]
            wup_b, wdn_b = cast[-2], cast[-1]

    y_sample = xs.reshape(bs, ss, d)
    return (xp, y_sample, jnp.stack(pool_p), jnp.stack(pool_s), jnp.stack(conv_p), jnp.stack(conv_s))
```

```python
import functools

import jax
import jax.numpy as jnp
from jax import lax
from jax.experimental import pallas as pl
from jax.experimental.pallas import tpu as pltpu

DEPTH = 4
PAST_LEN = 16384
POOL_WINDOWS = (2, 4, 8, 16)
N_GROUPS = len(POOL_WINDOWS)
POOL_STATE_LEN = max(POOL_WINDOWS) - 1
CONV_WIDTH = 3
CONV_STATE_LEN = CONV_WIDTH - 1
RMS_EPS = 1e-6

F32 = jnp.float32
BF16 = jnp.bfloat16

V7X_VMEM_BYTES = 64 * 1024 * 1024
VMEM_LIMIT = V7X_VMEM_BYTES - 2 * 1024 * 1024
F32_SUBLANES = 8
BF16_SUBLANES = 16
HALO = 2 * F32_SUBLANES

TILES = dict(
    ffn_tm=1024, ffn_tf=512, ffn_tn=512, ffn_pc=64,
    pool_tt=256, pool_bb=16,
    conv_tm=512, conv_tn=512,
    proj_tm=1024, proj_tn=1024,
)


def _params(sem):
    return pltpu.CompilerParams(dimension_semantics=sem, vmem_limit_bytes=VMEM_LIMIT)


def _rms(x, g):
    inv = lax.rsqrt(jnp.mean(x * x, axis=-1, keepdims=True) + RMS_EPS)
    return (x * inv) * g


ROW_CHUNK = 2 * BF16_SUBLANES


def _for_row_chunks(n_rows, fn):
    rc = min(ROW_CHUNK, n_rows)

    def body(r, carry):
        fn(pl.ds(pl.multiple_of(r * rc, rc), rc))
        return carry

    lax.fori_loop(0, n_rows // rc, body, 0)


FFN_LOAD_SLOTS = 4


def _ffn_kernel(*refs, n_cast, tm, tn, pc, final):
    x_hbm, xr_ref, g_ref, gf_ref, wup_ref, wdn_ref = refs[:6]
    cast_in = refs[6:6 + n_cast]
    o_hbm = refs[6 + n_cast]
    cast_out = refs[7 + n_cast:7 + 2 * n_cast]
    acc_ref, u_ref, xbuf_ref, in_sem, out_sem = refs[7 + 2 * n_cast:]
    i, j = pl.program_id(0), pl.program_id(1)
    ni, nj = pl.num_programs(0), pl.num_programs(1)
    n_chunks = tm // pc
    rr = xr_ref.shape[0]

    def tile_writeback(tile):
        rows = pl.ds(pl.multiple_of(tile * tm, tm), tm)
        return pltpu.make_async_copy(acc_ref, o_hbm.at[rows, :], out_sem.at[0])

    def chunk_load(c):
        slot = c % FFN_LOAD_SLOTS
        rows = pl.ds(pl.multiple_of(i * tm + c * pc, pc), pc)
        return pltpu.make_async_copy(x_hbm.at[rows, :], xbuf_ref.at[slot], in_sem.at[slot])

    @pl.when(j == 0)
    def _():
        for c in range(min(FFN_LOAD_SLOTS - 1, n_chunks)):
            chunk_load(c).start()

        def chunk_body(c, carry):
            @pl.when(c + FFN_LOAD_SLOTS - 1 < n_chunks)
            def _():
                chunk_load(c + FFN_LOAD_SLOTS - 1).start()

            chunk_load(c).wait()
            slot = c % FFN_LOAD_SLOTS
            for r in range(0, pc, ROW_CHUNK):
                rows = pl.ds(pl.multiple_of(c * pc + r, ROW_CHUNK), ROW_CHUNK)
                u_ref[rows, :] = _rms(xbuf_ref[slot, r:r + ROW_CHUNK, :], g_ref[...]).astype(BF16)
            return carry

        lax.fori_loop(0, n_chunks, chunk_body, 0)

        @pl.when(i > 0)
        def _():
            tile_writeback(i - 1).wait()

        def zero_rows(rows):
            acc_ref[rows, :] = jnp.zeros((rows.size, acc_ref.shape[1]), F32)

        _for_row_chunks(tm, zero_rows)

    h = jnp.dot(u_ref[...], wup_ref[...], preferred_element_type=F32)
    h = jnp.maximum(h, 0.0)
    h = (h * h).astype(BF16)
    for n in range(0, acc_ref.shape[1], tn):
        acc_ref[:, n:n + tn] += jnp.dot(h, wdn_ref[:, n:n + tn], preferred_element_type=F32)
    res_rows = pl.ds(pl.multiple_of(j * rr, rr), rr)
    acc_ref[res_rows, :] += xr_ref[...]

    for src, dst in zip(cast_in, cast_out):
        dst[...] = src[...].astype(BF16)

    @pl.when(j == nj - 1)
    def _():
        if final:
            def final_rows(rows):
                acc_ref[rows, :] = _rms(acc_ref[rows, :], gf_ref[...])

            _for_row_chunks(tm, final_rows)
        tile_writeback(i).start()

        @pl.when(i == ni - 1)
        def _():
            tile_writeback(i).wait()


def _cast_specs(w, layer, n_steps, nj):
    _, r, c = w.shape
    reps = 1
    while (r * reps) % n_steps or (r * reps // n_steps) % BF16_SUBLANES:
        reps *= 2
        assert reps <= n_steps, (w.shape, n_steps)
    rb = r * reps // n_steps
    in_spec = pl.BlockSpec((None, rb, c), lambda i, j: (layer, (i * nj + j) // reps, 0))
    out_spec = pl.BlockSpec((rb, c), lambda i, j: ((i * nj + j) // reps, 0))
    return in_spec, out_spec, jax.ShapeDtypeStruct((r, c), BF16)


def _ffn(x, g, gf, wup, wdn, cast_jobs=(), *, final):
    t, d = x.shape
    f = wup.shape[1]
    tm, tf, tn = min(TILES["ffn_tm"], t), min(TILES["ffn_tf"], f), min(TILES["ffn_tn"], d)
    pc = min(TILES["ffn_pc"], tm)
    ni, nj = t // tm, f // tf
    rr = tm // nj
    assert rr % F32_SUBLANES == 0 and pc % ROW_CHUNK == 0, (tm, nj, pc)
    cast = [_cast_specs(w, layer, ni * nj, nj) for w, layer in cast_jobs]
    outs = pl.pallas_call(
        functools.partial(_ffn_kernel, n_cast=len(cast), tm=tm, tn=tn, pc=pc, final=final),
        grid=(ni, nj),
        in_specs=[
            pl.BlockSpec(memory_space=pl.ANY),
            pl.BlockSpec((rr, d), lambda i, j: (i * nj + j, 0)),
            pl.BlockSpec((1, d), lambda i, j: (0, 0)),
            pl.BlockSpec((1, d), lambda i, j: (0, 0)),
            pl.BlockSpec((d, tf), lambda i, j: (0, j)),
            pl.BlockSpec((tf, d), lambda i, j: (j, 0)),
        ] + [c[0] for c in cast],
        out_specs=[pl.BlockSpec(memory_space=pl.ANY)] + [c[1] for c in cast],
        out_shape=[jax.ShapeDtypeStruct((t, d), F32)] + [c[2] for c in cast],
        scratch_shapes=[
            pltpu.VMEM((tm, d), F32),
            pltpu.VMEM((tm, d), BF16),
            pltpu.VMEM((FFN_LOAD_SLOTS, pc, d), F32),
            pltpu.SemaphoreType.DMA((FFN_LOAD_SLOTS,)),
            pltpu.SemaphoreType.DMA((1,)),
        ],
        compiler_params=_params(("arbitrary", "arbitrary")),
        name="ffn",
    )(x, x, g, gf, wup, wdn, *[w for w, _ in cast_jobs])
    return outs[0], list(outs[1:])


def _pool_prompt_kernel(x_ref, xh_ref, g_ref, w_ref, sc_ref, o_ref, st_ref, p_ref, *, tt):
    t = pl.program_id(1)
    g = g_ref[...]
    d = x_ref.shape[2]
    gd = d // N_GROUPS
    def norm_rows(rows):
        dst = pl.ds(pl.multiple_of(HALO + rows.start, HALO), rows.size)
        p_ref[dst, :] = _rms(x_ref[0, rows, :], g)

    _for_row_chunks(tt, norm_rows)

    @pl.when(t == 0)
    def _():
        p_ref[0:HALO, :] = jnp.zeros((HALO, d), F32)

    @pl.when(t > 0)
    def _():
        p_ref[0:HALO, :] = _rms(xh_ref[0], g)

    pos = t * tt + lax.broadcasted_iota(jnp.int32, (tt, 1), 0)
    for gi, w in enumerate(POOL_WINDOWS):
        sl = slice(gi * gd, (gi + 1) * gd)
        u = p_ref[HALO:HALO + tt, sl]
        s = u
        for k in range(1, w):
            s = s + p_ref[HALO - k:HALO - k + tt, sl]
        cnt = jnp.minimum(pos + 1, w).astype(F32)
        diff = (s / cnt - u).astype(BF16)
        y = jnp.dot(diff, w_ref[gi], preferred_element_type=F32) * sc_ref[:, sl]
        o_ref[0, :, sl] = x_ref[0, :, sl] + y
    st_ref[0] = p_ref[HALO + tt - POOL_STATE_LEN:HALO + tt, :]


def _pool_prompt(x, g, w, layer, sc):
    b, s, d = x.shape
    gd = d // N_GROUPS
    tt = min(TILES["pool_tt"], s)
    hb = tt // HALO
    return pl.pallas_call(
        functools.partial(_pool_prompt_kernel, tt=tt),
        grid=(b, s // tt),
        in_specs=[
            pl.BlockSpec((1, tt, d), lambda i, t: (i, t, 0)),
            pl.BlockSpec((1, HALO, d), lambda i, t: (i, jnp.maximum(t * hb - 1, 0), 0)),
            pl.BlockSpec((1, d), lambda i, t: (0, 0)),
            pl.BlockSpec((None, N_GROUPS, gd, gd), lambda i, t: (layer, 0, 0, 0), pipeline_mode=pl.Buffered(1)),
            pl.BlockSpec((1, d), lambda i, t: (0, 0)),
        ],
        out_specs=[
            pl.BlockSpec((1, tt, d), lambda i, t: (i, t, 0)),
            pl.BlockSpec((1, POOL_STATE_LEN, d), lambda i, t: (i, 0, 0)),
        ],
        out_shape=[
            jax.ShapeDtypeStruct((b, s, d), F32),
            jax.ShapeDtypeStruct((b, POOL_STATE_LEN, d), F32),
        ],
        scratch_shapes=[pltpu.VMEM((HALO + tt, d), F32)],
        compiler_params=_params(("arbitrary", "arbitrary")),
        name="pool_prompt",
    )(x, x, g, w, sc)


def _pool_sample_kernel(x_ref, s_ref, g_ref, w_ref, sc_ref, o_ref, so_ref, u_ref, d_ref, *, bb, n_t):
    d = u_ref.shape[2]
    gd = d // N_GROUPS
    g = g_ref[...]
    for t in range(n_t):
        u_ref[t] = _rms(x_ref[:, t * d:(t + 1) * d], g)

    def padded_row(idx, sl):
        if idx < POOL_STATE_LEN:
            return s_ref[:, idx * d + sl.start:idx * d + sl.stop]
        return u_ref[idx - POOL_STATE_LEN, :, sl]

    for gi, w in enumerate(POOL_WINDOWS):
        sl = slice(gi * gd, (gi + 1) * gd)
        for t in range(n_t):
            s = padded_row(POOL_STATE_LEN + t, sl)
            for k in range(1, w):
                s = s + padded_row(POOL_STATE_LEN + t - k, sl)
            cnt = float(min(PAST_LEN + t + 1, w))
            d_ref[t * bb:(t + 1) * bb, sl] = (s / cnt - u_ref[t, :, sl]).astype(BF16)
        y = jnp.dot(d_ref[:, sl], w_ref[gi], preferred_element_type=F32) * sc_ref[:, sl]
        for t in range(n_t):
            cs = slice(t * d + sl.start, t * d + sl.stop)
            o_ref[:, cs] = x_ref[:, cs] + y[t * bb:(t + 1) * bb]

    keep = POOL_STATE_LEN - n_t
    so_ref[:, 0:keep * d] = s_ref[:, n_t * d:POOL_STATE_LEN * d]
    for t in range(n_t):
        so_ref[:, (keep + t) * d:(keep + t + 1) * d] = u_ref[t]


def _pool_sample(x, state, g, w, layer, sc):
    b, td = x.shape
    d = g.shape[1]
    gd = d // N_GROUPS
    n_t = td // d
    assert n_t <= POOL_STATE_LEN
    sd = state.shape[1]
    bb = min(TILES["pool_bb"], b)
    return pl.pallas_call(
        functools.partial(_pool_sample_kernel, bb=bb, n_t=n_t),
        grid=(b // bb,),
        in_specs=[
            pl.BlockSpec((bb, td), lambda i: (i, 0)),
            pl.BlockSpec((bb, sd), lambda i: (i, 0)),
            pl.BlockSpec((1, d), lambda i: (0, 0)),
            pl.BlockSpec((None, N_GROUPS, gd, gd), lambda i: (layer, 0, 0, 0), pipeline_mode=pl.Buffered(1)),
            pl.BlockSpec((1, d), lambda i: (0, 0)),
        ],
        out_specs=[
            pl.BlockSpec((bb, td), lambda i: (i, 0)),
            pl.BlockSpec((bb, sd), lambda i: (i, 0)),
        ],
        out_shape=[
            jax.ShapeDtypeStruct((b, td), F32),
            jax.ShapeDtypeStruct((b, sd), F32),
        ],
        scratch_shapes=[pltpu.VMEM((n_t, bb, d), F32), pltpu.VMEM((n_t * bb, d), BF16)],
        compiler_params=_params(("arbitrary",)),
        name="pool_sample",
    )(x, state, g, w, sc)


def _conv_in_prompt_kernel(x_ref, g_ref, wb_ref, wc_ref, wh_ref, cw_ref, o_ref, st_ref,
                           u_ref, v_ref, carry_ref, *, tm):
    t = pl.program_id(1)
    c = pl.program_id(2)
    sub = F32_SUBLANES

    @pl.when(c == 0)
    def _():
        def norm_rows(rows):
            u_ref[rows, :] = _rms(x_ref[0, rows, :], g_ref[...]).astype(BF16)

        _for_row_chunks(tm, norm_rows)

    u = u_ref[...]
    zc = jnp.dot(u, wc_ref[...], preferred_element_type=F32)
    zh = jnp.dot(u, wh_ref[...], preferred_element_type=F32)
    v_ref[sub:sub + tm, :] = zc * zh

    @pl.when(t == 0)
    def _():
        v_ref[0:sub, :] = jnp.zeros((sub, v_ref.shape[1]), F32)

    @pl.when(t > 0)
    def _():
        v_ref[0:sub, :] = carry_ref[c]

    conv = cw_ref[CONV_WIDTH - 1:CONV_WIDTH, :] * v_ref[sub:sub + tm, :]
    for k in range(CONV_WIDTH - 1):
        off = sub - (CONV_WIDTH - 1) + k
        conv = conv + cw_ref[k:k + 1, :] * v_ref[off:off + tm, :]
    zb = jnp.dot(u, wb_ref[...], preferred_element_type=F32)
    o_ref[0] = (zb * conv).astype(BF16)
    carry_ref[c] = v_ref[tm:tm + sub, :]

    @pl.when(t == pl.num_programs(1) - 1)
    def _():
        st_ref[0, c] = v_ref[sub + tm - CONV_STATE_LEN:sub + tm, :]


def _conv_in_prompt(x, g, w_in, layer, cw):
    b, s, d = x.shape
    tm, tn = min(TILES["conv_tm"], s), min(TILES["conv_tn"], d)
    nc = d // tn
    gated, st = pl.pallas_call(
        functools.partial(_conv_in_prompt_kernel, tm=tm),
        grid=(b, s // tm, nc),
        in_specs=[
            pl.BlockSpec((1, tm, d), lambda i, t, c: (i, t, 0), pipeline_mode=pl.Buffered(1)),
            pl.BlockSpec((1, d), lambda i, t, c: (0, 0)),
            pl.BlockSpec((d, tn), lambda i, t, c: (0, c)),
            pl.BlockSpec((d, tn), lambda i, t, c: (0, nc + c)),
            pl.BlockSpec((d, tn), lambda i, t, c: (0, 2 * nc + c)),
            pl.BlockSpec((None, CONV_WIDTH, tn), lambda i, t, c: (layer, 0, c)),
        ],
        out_specs=[
            pl.BlockSpec((1, tm, tn), lambda i, t, c: (i, t, c)),
            pl.BlockSpec((1, nc, CONV_STATE_LEN, tn), lambda i, t, c: (i, 0, 0, 0)),
        ],
        out_shape=[
            jax.ShapeDtypeStruct((b, s, d), BF16),
            jax.ShapeDtypeStruct((b, nc, CONV_STATE_LEN, tn), F32),
        ],
        scratch_shapes=[
            pltpu.VMEM((tm, d), BF16),
            pltpu.VMEM((F32_SUBLANES + tm, tn), F32),
            pltpu.VMEM((nc, F32_SUBLANES, tn), F32),
        ],
        compiler_params=_params(("arbitrary", "arbitrary", "arbitrary")),
        name="conv_in_prompt",
    )(x, g, w_in, w_in, w_in, cw)
    return gated, st.transpose(0, 2, 1, 3).reshape(b, CONV_STATE_LEN, d)


def _conv_in_sample_kernel(x_ref, s0_ref, s1_ref, g_ref, wb_ref, wc_ref, wh_ref, cw_ref,
                           o_ref, st0_ref, st1_ref, u_ref, *, b, n_t):
    c = pl.program_id(0)
    d = u_ref.shape[1]

    @pl.when(c == 0)
    def _():
        for t in range(n_t):
            u_ref[t * b:(t + 1) * b, :] = _rms(x_ref[:, t * d:(t + 1) * d], g_ref[...]).astype(BF16)

    u = u_ref[...]
    zc = jnp.dot(u, wc_ref[...], preferred_element_type=F32)
    zh = jnp.dot(u, wh_ref[...], preferred_element_type=F32)
    zb = jnp.dot(u, wb_ref[...], preferred_element_type=F32)
    v = zc * zh
    rows = [s0_ref[...], s1_ref[...]] + [v[t * b:(t + 1) * b] for t in range(n_t)]
    for t in range(n_t):
        conv = cw_ref[0:1, :] * rows[t]
        for k in range(1, CONV_WIDTH):
            conv = conv + cw_ref[k:k + 1, :] * rows[t + k]
        o_ref[t * b:(t + 1) * b, :] = (zb[t * b:(t + 1) * b] * conv).astype(BF16)
    st0_ref[...] = rows[n_t]
    st1_ref[...] = rows[n_t + 1]


def _conv_in_sample(x, state, g, w_in, layer, cw):
    b, td = x.shape
    d = g.shape[1]
    n_t = td // d
    tn = min(TILES["conv_tn"], d)
    nc = d // tn
    assert CONV_STATE_LEN == 2
    return pl.pallas_call(
        functools.partial(_conv_in_sample_kernel, b=b, n_t=n_t),
        grid=(nc,),
        in_specs=[
            pl.BlockSpec((b, td), lambda c: (0, 0)),
            pl.BlockSpec((b, tn), lambda c: (0, c)),
            pl.BlockSpec((b, tn), lambda c: (0, nc + c)),
            pl.BlockSpec((1, d), lambda c: (0, 0)),
            pl.BlockSpec((d, tn), lambda c: (0, c)),
            pl.BlockSpec((d, tn), lambda c: (0, nc + c)),
            pl.BlockSpec((d, tn), lambda c: (0, 2 * nc + c)),
            pl.BlockSpec((None, CONV_WIDTH, tn), lambda c: (layer, 0, c)),
        ],
        out_specs=[
            pl.BlockSpec((n_t * b, tn), lambda c: (0, c)),
            pl.BlockSpec((b, tn), lambda c: (0, c)),
            pl.BlockSpec((b, tn), lambda c: (0, c)),
        ],
        out_shape=[
            jax.ShapeDtypeStruct((n_t * b, d), BF16),
            jax.ShapeDtypeStruct((b, d), F32),
            jax.ShapeDtypeStruct((b, d), F32),
        ],
        scratch_shapes=[pltpu.VMEM((n_t * b, d), BF16)],
        compiler_params=_params(("arbitrary",)),
        name="conv_in_sample",
    )(x, state, state, g, w_in, w_in, w_in, cw)


def _proj_res_kernel(a_ref, w_ref, x_ref, o_ref):
    o_ref[...] = x_ref[...] + jnp.dot(a_ref[...], w_ref[...], preferred_element_type=F32)


def _conv_out_prompt(a, w, x):
    t, d = x.shape
    tm, tn = min(TILES["proj_tm"], t), min(TILES["proj_tn"], d)
    return pl.pallas_call(
        _proj_res_kernel,
        grid=(t // tm, d // tn),
        in_specs=[
            pl.BlockSpec((tm, d), lambda i, n: (i, 0)),
            pl.BlockSpec((d, tn), lambda i, n: (0, n)),
            pl.BlockSpec((tm, tn), lambda i, n: (i, n)),
        ],
        out_specs=pl.BlockSpec((tm, tn), lambda i, n: (i, n)),
        out_shape=jax.ShapeDtypeStruct((t, d), F32),
        compiler_params=_params(("arbitrary", "arbitrary")),
        name="conv_out_prompt",
    )(a, w, x)


def _conv_out_sample(a, w, x):
    b, td = x.shape
    d = w.shape[0]
    n_t = td // d
    tn = min(TILES["proj_tn"], d)
    nn = d // tn
    return pl.pallas_call(
        _proj_res_kernel,
        grid=(n_t, nn),
        in_specs=[
            pl.BlockSpec((b, d), lambda t, n: (t, 0)),
            pl.BlockSpec((d, tn), lambda t, n: (0, n)),
            pl.BlockSpec((b, tn), lambda t, n: (0, t * nn + n)),
        ],
        out_specs=pl.BlockSpec((b, tn), lambda t, n: (0, t * nn + n)),
        out_shape=jax.ShapeDtypeStruct((b, td), F32),
        compiler_params=_params(("arbitrary", "arbitrary")),
        name="conv_out_sample",
    )(a, w, x)


def kernel(x_prompt, x_sample, state_pool, state_conv, norm_mix, norm_mlp, norm_final,
           w_pool, pool_scale, w_conv_in, conv_w, w_conv_out, w_up, w_down):
    bp, sp, d = x_prompt.shape
    bs, ss, _ = x_sample.shape

    w_pool_b = w_pool.astype(BF16)
    wup_b, wdn_b = w_up[0].astype(BF16), w_down[0].astype(BF16)
    wci_b = wco_b = None
    g_fin = norm_final.reshape(1, d)

    xp = x_prompt
    xs = x_sample.reshape(bs, ss * d)
    pool_p, pool_s, conv_p, conv_s = [], [], [], []
    for i in range(DEPTH):
        j = i // 2
        last = i == DEPTH - 1
        g_mix = norm_mix[i].reshape(1, d)
        if i % 2 == 0:
            sc = pool_scale[j].reshape(1, d)
            xp, st = _pool_prompt(xp, g_mix, w_pool_b, j, sc)
            pool_p.append(st)
            xs, st = _pool_sample(xs, state_pool[j].reshape(bs, POOL_STATE_LEN * d), g_mix, w_pool_b, j, sc)
            pool_s.append(st.reshape(bs, POOL_STATE_LEN, d))
        else:
            a, st = _conv_in_prompt(xp, g_mix, wci_b, j, conv_w)
            conv_p.append(st)
            xp = _conv_out_prompt(a.reshape(bp * sp, d), wco_b, xp.reshape(bp * sp, d)).reshape(bp, sp, d)
            a, st0, st1 = _conv_in_sample(xs, state_conv[j].reshape(bs, CONV_STATE_LEN * d), g_mix,
                                          wci_b, j, conv_w)
            conv_s.append(jnp.stack([st0, st1], axis=1))
            xs = _conv_out_sample(a, wco_b, xs)
        g_mlp = norm_mlp[i].reshape(1, d)
        jobs = []
        if not last:
            if (i + 1) % 2 == 1:
                jobs += [(w_conv_in, (i + 1) // 2), (w_conv_out, (i + 1) // 2)]
            jobs += [(w_up, i + 1), (w_down, i + 1)]
        xp, cast = _ffn(xp.reshape(bp * sp, d), g_mlp, g_fin, wup_b, wdn_b, jobs, final=last)
        xp = xp.reshape(bp, sp, d)
        xs, _ = _ffn(xs.reshape(bs * ss, d), g_mlp, g_fin, wup_b, wdn_b, final=last)
        xs = xs.reshape(bs, ss * d)
        if not last:
            if (i + 1) % 2 == 1:
                wci_b, wco_b = cast[0], cast[1]
            wup_b, wdn_b = cast[-2], cast[-1]

    y_sample = xs.reshape(bs, ss, d)
    return (xp, y_sample, jnp.stack(pool_p), jnp.stack(pool_s), jnp.stack(conv_p), jnp.stack(conv_s))
```

```python
import functools

import jax
import jax.numpy as jnp
from jax import lax
from jax.experimental import pallas as pl
from jax.experimental.pallas import tpu as pltpu

DEPTH = 4
PAST_LEN = 16384
POOL_WINDOWS = (2, 4, 8, 16)
N_GROUPS = len(POOL_WINDOWS)
POOL_STATE_LEN = max(POOL_WINDOWS) - 1
CONV_WIDTH = 3
CONV_STATE_LEN = CONV_WIDTH - 1
RMS_EPS = 1e-6

F32 = jnp.float32
BF16 = jnp.bfloat16

V7X_VMEM_BYTES = 64 * 1024 * 1024
VMEM_LIMIT = V7X_VMEM_BYTES - 2 * 1024 * 1024
F32_SUBLANES = 8
BF16_SUBLANES = 16
HALO = 2 * F32_SUBLANES
assert POOL_STATE_LEN <= HALO and all(w & (w - 1) == 0 for w in POOL_WINDOWS)

TILES = dict(
    ffn_tm=1024, ffn_tf=512, ffn_tn=512, load_pc=64,
    pool_tt=256, pool_bb=16,
    conv_tm=1024, conv_tn=512,
    proj_tm=1024, proj_tn=1024,
)


def _params(sem):
    return pltpu.CompilerParams(dimension_semantics=sem, vmem_limit_bytes=VMEM_LIMIT)


def _rms(x, g):
    inv = lax.rsqrt(jnp.mean(x * x, axis=-1, keepdims=True) + RMS_EPS)
    return (x * inv) * g


ROW_CHUNK = 2 * BF16_SUBLANES


def _for_row_chunks(n_rows, fn):
    rc = min(ROW_CHUNK, n_rows)

    def body(r, carry):
        fn(pl.ds(pl.multiple_of(r * rc, rc), rc))
        return carry

    lax.fori_loop(0, n_rows // rc, body, 0)


LOAD_SLOTS = 4


def _load_rows_normed(src_rows, u_ref, g_ref, xbuf_ref, sem, n_chunks, pc):
    def copy(c):
        slot = c % LOAD_SLOTS
        return pltpu.make_async_copy(src_rows(c), xbuf_ref.at[slot], sem.at[slot])

    for c in range(min(LOAD_SLOTS - 1, n_chunks)):
        copy(c).start()

    def chunk_body(c, carry):
        @pl.when(c + LOAD_SLOTS - 1 < n_chunks)
        def _():
            copy(c + LOAD_SLOTS - 1).start()

        copy(c).wait()
        slot = c % LOAD_SLOTS
        for r in range(0, pc, ROW_CHUNK):
            rows = pl.ds(pl.multiple_of(c * pc + r, ROW_CHUNK), ROW_CHUNK)
            u_ref[rows, :] = _rms(xbuf_ref[slot, r:r + ROW_CHUNK, :], g_ref[...]).astype(BF16)
        return carry

    lax.fori_loop(0, n_chunks, chunk_body, 0)


def _ffn_kernel(*refs, n_cast, tm, tn, pc, final):
    x_hbm, xr_ref, g_ref, gf_ref, wup_ref, wdn_ref = refs[:6]
    cast_in = refs[6:6 + n_cast]
    o_hbm = refs[6 + n_cast]
    cast_out = refs[7 + n_cast:7 + 2 * n_cast]
    acc_ref, u_ref, xbuf_ref, in_sem, out_sem = refs[7 + 2 * n_cast:]
    i, j = pl.program_id(0), pl.program_id(1)
    ni, nj = pl.num_programs(0), pl.num_programs(1)
    rr = xr_ref.shape[0]

    def tile_writeback(tile):
        rows = pl.ds(pl.multiple_of(tile * tm, tm), tm)
        return pltpu.make_async_copy(acc_ref, o_hbm.at[rows, :], out_sem.at[0])

    @pl.when(j == 0)
    def _():
        def src_rows(c):
            return x_hbm.at[pl.ds(pl.multiple_of(i * tm + c * pc, pc), pc), :]

        _load_rows_normed(src_rows, u_ref, g_ref, xbuf_ref, in_sem, tm // pc, pc)

        @pl.when(i > 0)
        def _():
            tile_writeback(i - 1).wait()

        def zero_rows(rows):
            acc_ref[rows, :] = jnp.zeros((rows.size, acc_ref.shape[1]), F32)

        _for_row_chunks(tm, zero_rows)

    h = jnp.dot(u_ref[...], wup_ref[...], preferred_element_type=F32)
    h = jnp.maximum(h, 0.0)
    h = (h * h).astype(BF16)
    casts = list(zip(cast_in, cast_out))
    for k, n in enumerate(range(0, acc_ref.shape[1], tn)):
        acc_ref[:, n:n + tn] += jnp.dot(h, wdn_ref[:, n:n + tn], preferred_element_type=F32)
        if k < len(casts):
            casts[k][1][...] = casts[k][0][...].astype(BF16)
    for src, dst in casts[acc_ref.shape[1] // tn:]:
        dst[...] = src[...].astype(BF16)
    res_rows = pl.ds(pl.multiple_of(j * rr, rr), rr)
    acc_ref[res_rows, :] += xr_ref[...]

    @pl.when(j == nj - 1)
    def _():
        if final:
            def final_rows(rows):
                acc_ref[rows, :] = _rms(acc_ref[rows, :], gf_ref[...])

            _for_row_chunks(tm, final_rows)
        tile_writeback(i).start()

        @pl.when(i == ni - 1)
        def _():
            tile_writeback(i).wait()


def _cast_specs(w, layer, n_steps, nj):
    _, r, c = w.shape
    reps = 1
    while (r * reps) % n_steps or (r * reps // n_steps) % BF16_SUBLANES:
        reps *= 2
        assert reps <= n_steps, (w.shape, n_steps)
    rb = r * reps // n_steps
    in_spec = pl.BlockSpec((None, rb, c), lambda i, j: (layer, (i * nj + j) // reps, 0))
    out_spec = pl.BlockSpec((rb, c), lambda i, j: ((i * nj + j) // reps, 0))
    return in_spec, out_spec, jax.ShapeDtypeStruct((r, c), BF16)


def _ffn(x, g, gf, wup, wdn, cast_jobs=(), *, final):
    t, d = x.shape
    f = wup.shape[1]
    tm, tf, tn = min(TILES["ffn_tm"], t), min(TILES["ffn_tf"], f), min(TILES["ffn_tn"], d)
    pc = min(TILES["load_pc"], tm)
    ni, nj = t // tm, f // tf
    rr = tm // nj
    assert rr % F32_SUBLANES == 0 and pc % ROW_CHUNK == 0, (tm, nj, pc)
    cast = [_cast_specs(w, layer, ni * nj, nj) for w, layer in cast_jobs]
    outs = pl.pallas_call(
        functools.partial(_ffn_kernel, n_cast=len(cast), tm=tm, tn=tn, pc=pc, final=final),
        grid=(ni, nj),
        in_specs=[
            pl.BlockSpec(memory_space=pl.ANY),
            pl.BlockSpec((rr, d), lambda i, j: (i * nj + j, 0)),
            pl.BlockSpec((1, d), lambda i, j: (0, 0)),
            pl.BlockSpec((1, d), lambda i, j: (0, 0)),
            pl.BlockSpec((d, tf), lambda i, j: (0, j)),
            pl.BlockSpec((tf, d), lambda i, j: (j, 0)),
        ] + [c[0] for c in cast],
        out_specs=[pl.BlockSpec(memory_space=pl.ANY)] + [c[1] for c in cast],
        out_shape=[jax.ShapeDtypeStruct((t, d), F32)] + [c[2] for c in cast],
        scratch_shapes=[
            pltpu.VMEM((tm, d), F32),
            pltpu.VMEM((tm, d), BF16),
            pltpu.VMEM((LOAD_SLOTS, pc, d), F32),
            pltpu.SemaphoreType.DMA((LOAD_SLOTS,)),
            pltpu.SemaphoreType.DMA((1,)),
        ],
        compiler_params=_params(("arbitrary", "arbitrary")),
        name="ffn",
    )(x, x, g, gf, wup, wdn, *[w for w, _ in cast_jobs])
    return outs[0], list(outs[1:])


def _pool_prompt_kernel(x_ref, xh_ref, g_ref, w_ref, sc_ref, o_ref, st_ref, p_ref, *, tt):
    t = pl.program_id(1)
    g = g_ref[...]
    d = x_ref.shape[2]
    gd = d // N_GROUPS
    for r in range(0, tt, ROW_CHUNK):
        p_ref[HALO + r:HALO + r + ROW_CHUNK, :] = _rms(x_ref[0, r:r + ROW_CHUNK, :], g)

    @pl.when(t == 0)
    def _():
        p_ref[0:HALO, :] = jnp.zeros((HALO, d), F32)

    @pl.when(t > 0)
    def _():
        p_ref[0:HALO, :] = _rms(xh_ref[0], g)

    pos = t * tt + lax.broadcasted_iota(jnp.int32, (tt, 1), 0)
    for gi, w in enumerate(POOL_WINDOWS):
        sl = slice(gi * gd, (gi + 1) * gd)
        u = p_ref[HALO:HALO + tt, sl]
        s = p_ref[0:HALO + tt, sl]
        k = 1
        while k < w:
            s = s + pltpu.roll(s, k, 0)
            k *= 2
        s = s[HALO:HALO + tt]
        cnt = jnp.minimum(pos + 1, w).astype(F32)
        diff = (s / cnt - u).astype(BF16)
        y = jnp.dot(diff, w_ref[gi], preferred_element_type=F32) * sc_ref[:, sl]
        o_ref[0, :, sl] = x_ref[0, :, sl] + y
    st_ref[0] = p_ref[HALO + tt - POOL_STATE_LEN:HALO + tt, :]


def _pool_prompt(x, g, w, layer, sc):
    b, s, d = x.shape
    gd = d // N_GROUPS
    tt = min(TILES["pool_tt"], s)
    hb = tt // HALO
    return pl.pallas_call(
        functools.partial(_pool_prompt_kernel, tt=tt),
        grid=(b, s // tt),
        in_specs=[
            pl.BlockSpec((1, tt, d), lambda i, t: (i, t, 0)),
            pl.BlockSpec((1, HALO, d), lambda i, t: (i, jnp.maximum(t * hb - 1, 0), 0)),
            pl.BlockSpec((1, d), lambda i, t: (0, 0)),
            pl.BlockSpec((None, N_GROUPS, gd, gd), lambda i, t: (layer, 0, 0, 0), pipeline_mode=pl.Buffered(1)),
            pl.BlockSpec((1, d), lambda i, t: (0, 0)),
        ],
        out_specs=[
            pl.BlockSpec((1, tt, d), lambda i, t: (i, t, 0)),
            pl.BlockSpec((1, POOL_STATE_LEN, d), lambda i, t: (i, 0, 0)),
        ],
        out_shape=[
            jax.ShapeDtypeStruct((b, s, d), F32),
            jax.ShapeDtypeStruct((b, POOL_STATE_LEN, d), F32),
        ],
        scratch_shapes=[pltpu.VMEM((HALO + tt, d), F32)],
        compiler_params=_params(("arbitrary", "arbitrary")),
        name="pool_prompt",
    )(x, x, g, w, sc)


def _pool_sample_kernel(x_ref, s_ref, g_ref, w_ref, sc_ref, *rest, bb, n_t, layer, first):
    o_ref, so_all_ref, u_ref, d_ref = rest[-4:]
    if first:
        so_ref = so_all_ref.at[layer]
        for other in range(so_all_ref.shape[0]):
            if other != layer:
                so_all_ref[other] = jnp.zeros(so_all_ref.shape[1:], F32)
    else:
        so_ref = so_all_ref
    d = u_ref.shape[2]
    gd = d // N_GROUPS
    g = g_ref[...]
    for t in range(n_t):
        u_ref[t] = _rms(x_ref[:, t * d:(t + 1) * d], g)

    def padded_row(idx, sl):
        if idx < POOL_STATE_LEN:
            return s_ref[:, idx * d + sl.start:idx * d + sl.stop]
        return u_ref[idx - POOL_STATE_LEN, :, sl]

    for gi, w in enumerate(POOL_WINDOWS):
        sl = slice(gi * gd, (gi + 1) * gd)
        for t in range(n_t):
            s = padded_row(POOL_STATE_LEN + t, sl)
            for k in range(1, w):
                s = s + padded_row(POOL_STATE_LEN + t - k, sl)
            cnt = float(min(PAST_LEN + t + 1, w))
            d_ref[t * bb:(t + 1) * bb, sl] = (s / cnt - u_ref[t, :, sl]).astype(BF16)
        y = jnp.dot(d_ref[:, sl], w_ref[gi], preferred_element_type=F32) * sc_ref[:, sl]
        for t in range(n_t):
            cs = slice(t * d + sl.start, t * d + sl.stop)
            o_ref[:, cs] = x_ref[:, cs] + y[t * bb:(t + 1) * bb]

    keep = POOL_STATE_LEN - n_t
    for k in range(keep):
        so_ref[:, k, :] = s_ref[:, (n_t + k) * d:(n_t + k + 1) * d]
    for t in range(n_t):
        so_ref[:, keep + t, :] = u_ref[t]


def _pool_sample(x, state, g, w, layer, sc, states_out):
    b, td = x.shape
    d = g.shape[1]
    gd = d // N_GROUPS
    n_t = td // d
    assert n_t <= POOL_STATE_LEN
    sd = state.shape[1]
    bb = min(TILES["pool_bb"], b)
    prev = [] if states_out is None else [states_out]
    n_layers = w.shape[0]
    if prev:
        so_spec = pl.BlockSpec((None, bb, POOL_STATE_LEN, d), lambda i: (layer, i, 0, 0))
    else:
        so_spec = pl.BlockSpec((n_layers, bb, POOL_STATE_LEN, d), lambda i: (0, i, 0, 0))
    return pl.pallas_call(
        functools.partial(_pool_sample_kernel, bb=bb, n_t=n_t, layer=layer, first=not prev),
        grid=(b // bb,),
        in_specs=[
            pl.BlockSpec((bb, td), lambda i: (i, 0)),
            pl.BlockSpec((bb, sd), lambda i: (i, 0)),
            pl.BlockSpec((1, d), lambda i: (0, 0)),
            pl.BlockSpec((None, N_GROUPS, gd, gd), lambda i: (layer, 0, 0, 0), pipeline_mode=pl.Buffered(1)),
            pl.BlockSpec((1, d), lambda i: (0, 0)),
        ] + [pl.BlockSpec(memory_space=pl.ANY) for _ in prev],
        out_specs=[
            pl.BlockSpec((bb, td), lambda i: (i, 0)),
            so_spec,
        ],
        out_shape=[
            jax.ShapeDtypeStruct((b, td), F32),
            jax.ShapeDtypeStruct((n_layers, b, POOL_STATE_LEN, d), F32),
        ],
        input_output_aliases={5: 1} if prev else {},
        scratch_shapes=[pltpu.VMEM((n_t, bb, d), F32), pltpu.VMEM((n_t * bb, d), BF16)],
        compiler_params=_params(("arbitrary",)),
        name="pool_sample",
    )(x, state, g, w, sc, *prev)


def _conv_in_prompt_kernel(x_hbm, g_ref, wb_ref, wc_ref, wh_ref, cw_ref, o_ref, st_ref,
                           u_ref, v_ref, carry_ref, xbuf_ref, in_sem, *, tm, pc):
    i = pl.program_id(0)
    t = pl.program_id(1)
    c = pl.program_id(2)
    sub = F32_SUBLANES

    @pl.when(c == 0)
    def _():
        def src_rows(k):
            return x_hbm.at[i, pl.ds(pl.multiple_of(t * tm + k * pc, pc), pc), :]

        _load_rows_normed(src_rows, u_ref, g_ref, xbuf_ref, in_sem, tm // pc, pc)

    u = u_ref[...]
    zc = jnp.dot(u, wc_ref[...], preferred_element_type=F32)
    zh = jnp.dot(u, wh_ref[...], preferred_element_type=F32)
    v_ref[sub:sub + tm, :] = zc * zh

    @pl.when(t == 0)
    def _():
        v_ref[0:sub, :] = jnp.zeros((sub, v_ref.shape[1]), F32)

    @pl.when(t > 0)
    def _():
        v_ref[0:sub, :] = carry_ref[c]

    conv = cw_ref[CONV_WIDTH - 1:CONV_WIDTH, :] * v_ref[sub:sub + tm, :]
    for k in range(CONV_WIDTH - 1):
        off = sub - (CONV_WIDTH - 1) + k
        conv = conv + cw_ref[k:k + 1, :] * v_ref[off:off + tm, :]
    zb = jnp.dot(u, wb_ref[...], preferred_element_type=F32)
    o_ref[0] = (zb * conv).astype(BF16)
    carry_ref[c] = v_ref[tm:tm + sub, :]

    @pl.when(t == pl.num_programs(1) - 1)
    def _():
        st_ref[0, c] = v_ref[sub + tm - CONV_STATE_LEN:sub + tm, :]


def _conv_in_prompt(x, g, w_in, layer, cw):
    b, s, d = x.shape
    tm, tn = min(TILES["conv_tm"], s), min(TILES["conv_tn"], d)
    pc = min(TILES["load_pc"], tm)
    nc = d // tn
    gated, st = pl.pallas_call(
        functools.partial(_conv_in_prompt_kernel, tm=tm, pc=pc),
        grid=(b, s // tm, nc),
        in_specs=[
            pl.BlockSpec(memory_space=pl.ANY),
            pl.BlockSpec((1, d), lambda i, t, c: (0, 0)),
            pl.BlockSpec((d, tn), lambda i, t, c: (0, c)),
            pl.BlockSpec((d, tn), lambda i, t, c: (0, nc + c)),
            pl.BlockSpec((d, tn), lambda i, t, c: (0, 2 * nc + c)),
            pl.BlockSpec((None, CONV_WIDTH, tn), lambda i, t, c: (layer, 0, c)),
        ],
        out_specs=[
            pl.BlockSpec((1, tm, tn), lambda i, t, c: (i, t, c)),
            pl.BlockSpec((1, nc, CONV_STATE_LEN, tn), lambda i, t, c: (i, 0, 0, 0)),
        ],
        out_shape=[
            jax.ShapeDtypeStruct((b, s, d), BF16),
            jax.ShapeDtypeStruct((b, nc, CONV_STATE_LEN, tn), F32),
        ],
        scratch_shapes=[
            pltpu.VMEM((tm, d), BF16),
            pltpu.VMEM((F32_SUBLANES + tm, tn), F32),
            pltpu.VMEM((nc, F32_SUBLANES, tn), F32),
            pltpu.VMEM((LOAD_SLOTS, pc, d), F32),
            pltpu.SemaphoreType.DMA((LOAD_SLOTS,)),
        ],
        compiler_params=_params(("arbitrary", "arbitrary", "arbitrary")),
        name="conv_in_prompt",
    )(x, g, w_in, w_in, w_in, cw)
    return gated, st.transpose(0, 2, 1, 3).reshape(b, CONV_STATE_LEN, d)


def _conv_in_sample_kernel(x_ref, s0_ref, s1_ref, g_ref, wb_ref, wc_ref, wh_ref, cw_ref,
                           o_ref, st0_ref, st1_ref, u_ref, *, b, n_t):
    c = pl.program_id(0)
    d = u_ref.shape[1]

    @pl.when(c == 0)
    def _():
        for t in range(n_t):
            u_ref[t * b:(t + 1) * b, :] = _rms(x_ref[:, t * d:(t + 1) * d], g_ref[...]).astype(BF16)

    u = u_ref[...]
    zc = jnp.dot(u, wc_ref[...], preferred_element_type=F32)
    zh = jnp.dot(u, wh_ref[...], preferred_element_type=F32)
    zb = jnp.dot(u, wb_ref[...], preferred_element_type=F32)
    v = zc * zh
    rows = [s0_ref[...], s1_ref[...]] + [v[t * b:(t + 1) * b] for t in range(n_t)]
    for t in range(n_t):
        conv = cw_ref[0:1, :] * rows[t]
        for k in range(1, CONV_WIDTH):
            conv = conv + cw_ref[k:k + 1, :] * rows[t + k]
        o_ref[t * b:(t + 1) * b, :] = (zb[t * b:(t + 1) * b] * conv).astype(BF16)
    st0_ref[...] = rows[n_t]
    st1_ref[...] = rows[n_t + 1]


def _conv_in_sample(x, state, g, w_in, layer, cw):
    b, td = x.shape
    d = g.shape[1]
    n_t = td // d
    tn = min(TILES["conv_tn"], d)
    nc = d // tn
    assert CONV_STATE_LEN == 2
    return pl.pallas_call(
        functools.partial(_conv_in_sample_kernel, b=b, n_t=n_t),
        grid=(nc,),
        in_specs=[
            pl.BlockSpec((b, td), lambda c: (0, 0)),
            pl.BlockSpec((b, tn), lambda c: (0, c)),
            pl.BlockSpec((b, tn), lambda c: (0, nc + c)),
            pl.BlockSpec((1, d), lambda c: (0, 0)),
            pl.BlockSpec((d, tn), lambda c: (0, c)),
            pl.BlockSpec((d, tn), lambda c: (0, nc + c)),
            pl.BlockSpec((d, tn), lambda c: (0, 2 * nc + c)),
            pl.BlockSpec((None, CONV_WIDTH, tn), lambda c: (layer, 0, c)),
        ],
        out_specs=[
            pl.BlockSpec((n_t * b, tn), lambda c: (0, c)),
            pl.BlockSpec((b, tn), lambda c: (0, c)),
            pl.BlockSpec((b, tn), lambda c: (0, c)),
        ],
        out_shape=[
            jax.ShapeDtypeStruct((n_t * b, d), BF16),
            jax.ShapeDtypeStruct((b, d), F32),
            jax.ShapeDtypeStruct((b, d), F32),
        ],
        scratch_shapes=[pltpu.VMEM((n_t * b, d), BF16)],
        compiler_params=_params(("arbitrary",)),
        name="conv_in_sample",
    )(x, state, state, g, w_in, w_in, w_in, cw)


def _proj_res_kernel(a_ref, w_ref, x_ref, o_ref):
    o_ref[...] = x_ref[...] + jnp.dot(a_ref[...], w_ref[...], preferred_element_type=F32)


def _conv_out_prompt(a, w, x):
    t, d = x.shape
    tm, tn = min(TILES["proj_tm"], t), min(TILES["proj_tn"], d)
    return pl.pallas_call(
        _proj_res_kernel,
        grid=(t // tm, d // tn),
        in_specs=[
            pl.BlockSpec((tm, d), lambda i, n: (i, 0)),
            pl.BlockSpec((d, tn), lambda i, n: (0, n)),
            pl.BlockSpec((tm, tn), lambda i, n: (i, n)),
        ],
        out_specs=pl.BlockSpec((tm, tn), lambda i, n: (i, n)),
        out_shape=jax.ShapeDtypeStruct((t, d), F32),
        compiler_params=_params(("arbitrary", "arbitrary")),
        name="conv_out_prompt",
    )(a, w, x)


def _conv_out_sample(a, w, x):
    b, td = x.shape
    d = w.shape[0]
    n_t = td // d
    tn = min(TILES["proj_tn"], d)
    nn = d // tn
    return pl.pallas_call(
        _proj_res_kernel,
        grid=(n_t, nn),
        in_specs=[
            pl.BlockSpec((b, d), lambda t, n: (t, 0)),
            pl.BlockSpec((d, tn), lambda t, n: (0, n)),
            pl.BlockSpec((b, tn), lambda t, n: (0, t * nn + n)),
        ],
        out_specs=pl.BlockSpec((b, tn), lambda t, n: (0, t * nn + n)),
        out_shape=jax.ShapeDtypeStruct((b, td), F32),
        compiler_params=_params(("arbitrary", "arbitrary")),
        name="conv_out_sample",
    )(a, w, x)


def kernel(x_prompt, x_sample, state_pool, state_conv, norm_mix, norm_mlp, norm_final,
           w_pool, pool_scale, w_conv_in, conv_w, w_conv_out, w_up, w_down):
    bp, sp, d = x_prompt.shape
    bs, ss, _ = x_sample.shape

    w_pool_b = w_pool.astype(BF16)
    wup_b, wdn_b = w_up[0].astype(BF16), w_down[0].astype(BF16)
    wci_b = wco_b = None
    g_fin = norm_final.reshape(1, d)

    xp = x_prompt
    xs = x_sample.reshape(bs, ss * d)
    pool_p, conv_p, conv_s = [], [], []
    pool_s = None
    for i in range(DEPTH):
        j = i // 2
        last = i == DEPTH - 1
        g_mix = norm_mix[i].reshape(1, d)
        if i % 2 == 0:
            sc = pool_scale[j].reshape(1, d)
            xp, st = _pool_prompt(xp, g_mix, w_pool_b, j, sc)
            pool_p.append(st)
            xs, pool_s = _pool_sample(xs, state_pool[j].reshape(bs, POOL_STATE_LEN * d), g_mix, w_pool_b, j, sc,
                                      pool_s)
        else:
            a, st = _conv_in_prompt(xp, g_mix, wci_b, j, conv_w)
            conv_p.append(st)
            xp = _conv_out_prompt(a.reshape(bp * sp, d), wco_b, xp.reshape(bp * sp, d)).reshape(bp, sp, d)
            a, st0, st1 = _conv_in_sample(xs, state_conv[j].reshape(bs, CONV_STATE_LEN * d), g_mix,
                                          wci_b, j, conv_w)
            conv_s.append(jnp.stack([st0, st1], axis=1))
            xs = _conv_out_sample(a, wco_b, xs)
        g_mlp = norm_mlp[i].reshape(1, d)
        jobs = []
        if not last:
            if (i + 1) % 2 == 1:
                jobs += [(w_conv_in, (i + 1) // 2), (w_conv_out, (i + 1) // 2)]
            jobs += [(w_up, i + 1), (w_down, i + 1)]
        xp, cast = _ffn(xp.reshape(bp * sp, d), g_mlp, g_fin, wup_b, wdn_b, jobs, final=last)
        xp = xp.reshape(bp, sp, d)
        xs, _ = _ffn(xs.reshape(bs * ss, d), g_mlp, g_fin, wup_b, wdn_b, final=last)
        xs = xs.reshape(bs, ss * d)
        if not last:
            if (i + 1) % 2 == 1:
                wci_b, wco_b = cast[0], cast[1]
            wup_b, wdn_b = cast[-2], cast[-1]

    y_sample = xs.reshape(bs, ss, d)
    return (xp, y_sample, jnp.stack(pool_p), pool_s, jnp.stack(conv_p), jnp.stack(conv_s))
```

```python
import functools

import jax
import jax.numpy as jnp
from jax import lax
from jax.experimental import pallas as pl
from jax.experimental.pallas import tpu as pltpu

DEPTH = 4
PAST_LEN = 16384
POOL_WINDOWS = (2, 4, 8, 16)
N_GROUPS = len(POOL_WINDOWS)
POOL_STATE_LEN = max(POOL_WINDOWS) - 1
CONV_WIDTH = 3
CONV_STATE_LEN = CONV_WIDTH - 1
RMS_EPS = 1e-6

F32 = jnp.float32
BF16 = jnp.bfloat16

V7X_VMEM_BYTES = 64 * 1024 * 1024
VMEM_LIMIT = V7X_VMEM_BYTES - 1024 * 1024
F32_SUBLANES = 8
BF16_SUBLANES = 16
HALO = 2 * F32_SUBLANES
assert POOL_STATE_LEN <= HALO and all(w & (w - 1) == 0 for w in POOL_WINDOWS)

TILES = dict(
    ffn_tm=1024, ffn_tf=512, ffn_tn=512, load_pc=32,
    pool_tt=256, pool_bb=16,
    conv_tm=1024, conv_tn=512,
    proj_tm=1024, proj_tn=1024,
)


def _params(sem):
    return pltpu.CompilerParams(dimension_semantics=sem, vmem_limit_bytes=VMEM_LIMIT)


def _rms(x, g):
    inv = lax.rsqrt(jnp.mean(x * x, axis=-1, keepdims=True) + RMS_EPS)
    return (x * inv) * g


ROW_CHUNK = 2 * BF16_SUBLANES


def _for_row_chunks(n_rows, fn):
    rc = min(ROW_CHUNK, n_rows)

    def body(r, carry):
        fn(pl.ds(pl.multiple_of(r * rc, rc), rc))
        return carry

    lax.fori_loop(0, n_rows // rc, body, 0)


LOAD_SLOTS = 3


def _load_rows_normed(src_rows, u_ref, g_ref, xbuf_ref, sem, n_chunks, pc):
    def copy(c):
        slot = c % LOAD_SLOTS
        return pltpu.make_async_copy(src_rows(c), xbuf_ref.at[slot], sem.at[slot])

    for c in range(min(LOAD_SLOTS - 1, n_chunks)):
        copy(c).start()

    def chunk_body(c, carry):
        @pl.when(c + LOAD_SLOTS - 1 < n_chunks)
        def _():
            copy(c + LOAD_SLOTS - 1).start()

        copy(c).wait()
        slot = c % LOAD_SLOTS
        for r in range(0, pc, ROW_CHUNK):
            rows = pl.ds(pl.multiple_of(c * pc + r, ROW_CHUNK), ROW_CHUNK)
            u_ref[rows, :] = _rms(xbuf_ref[slot, r:r + ROW_CHUNK, :], g_ref[...]).astype(BF16)
        return carry

    lax.fori_loop(0, n_chunks, chunk_body, 0)


def _ffn_kernel(*refs, n_cast, tm, tn, pc, final):
    x_hbm, xr_ref, xn_ref, g_ref, gf_ref, wup_ref, wdn_ref = refs[:7]
    cast_in = refs[7:7 + n_cast]
    o_hbm = refs[7 + n_cast]
    cast_out = refs[8 + n_cast:8 + 2 * n_cast]
    acc_ref, u_ref, xbuf_ref, in_sem, out_sem = refs[8 + 2 * n_cast:]
    i, j = pl.program_id(0), pl.program_id(1)
    ni, nj = pl.num_programs(0), pl.num_programs(1)
    rr = xr_ref.shape[0]

    def tile_writeback(tile):
        rows = pl.ds(pl.multiple_of(tile * tm, tm), tm)
        return pltpu.make_async_copy(acc_ref, o_hbm.at[rows, :], out_sem.at[0])

    @pl.when(j == 0)
    def _():
        @pl.when(i == 0)
        def _():
            def src_rows(c):
                return x_hbm.at[pl.ds(pl.multiple_of(c * pc, pc), pc), :]

            _load_rows_normed(src_rows, u_ref.at[0], g_ref, xbuf_ref, in_sem, tm // pc, pc)

        @pl.when(i > 0)
        def _():
            tile_writeback(i - 1).wait()

        def zero_rows(rows):
            acc_ref[rows, :] = jnp.zeros((rows.size, acc_ref.shape[1]), F32)

        _for_row_chunks(tm, zero_rows)

    h = jnp.dot(u_ref[i % 2], wup_ref[...], preferred_element_type=F32)
    h = jnp.maximum(h, 0.0)
    h = (h * h).astype(BF16)
    casts = list(zip(cast_in, cast_out))
    for k, n in enumerate(range(0, acc_ref.shape[1], tn)):
        acc_ref[:, n:n + tn] += jnp.dot(h, wdn_ref[:, n:n + tn], preferred_element_type=F32)
        if k < len(casts):
            casts[k][1][...] = casts[k][0][...].astype(BF16)
    for src, dst in casts[acc_ref.shape[1] // tn:]:
        dst[...] = src[...].astype(BF16)
    slab = pl.ds(pl.multiple_of(j * rr, rr), rr)
    acc_ref[slab, :] += xr_ref[...]
    u_ref[(i + 1) % 2, slab, :] = _rms(xn_ref[...], g_ref[...]).astype(BF16)

    @pl.when(j == nj - 1)
    def _():
        if final:
            def final_rows(rows):
                acc_ref[rows, :] = _rms(acc_ref[rows, :], gf_ref[...])

            _for_row_chunks(tm, final_rows)
        tile_writeback(i).start()

        @pl.when(i == ni - 1)
        def _():
            tile_writeback(i).wait()


def _cast_specs(w, layer, n_steps, nj):
    _, r, c = w.shape
    reps = 1
    while (r * reps) % n_steps or (r * reps // n_steps) % BF16_SUBLANES:
        reps *= 2
        assert reps <= n_steps, (w.shape, n_steps)
    rb = r * reps // n_steps
    in_spec = pl.BlockSpec((None, rb, c), lambda i, j: (layer, (i * nj + j) // reps, 0))
    out_spec = pl.BlockSpec((rb, c), lambda i, j: ((i * nj + j) // reps, 0))
    return in_spec, out_spec, jax.ShapeDtypeStruct((r, c), BF16)


def _ffn(x, g, gf, wup, wdn, cast_jobs=(), *, final):
    t, d = x.shape
    f = wup.shape[1]
    tm, tf, tn = min(TILES["ffn_tm"], t), min(TILES["ffn_tf"], f), min(TILES["ffn_tn"], d)
    pc = min(TILES["load_pc"], tm)
    ni, nj = t // tm, f // tf
    rr = tm // nj
    assert rr % BF16_SUBLANES == 0 and pc % ROW_CHUNK == 0, (tm, nj, pc)
    cast = [_cast_specs(w, layer, ni * nj, nj) for w, layer in cast_jobs]
    outs = pl.pallas_call(
        functools.partial(_ffn_kernel, n_cast=len(cast), tm=tm, tn=tn, pc=pc, final=final),
        grid=(ni, nj),
        in_specs=[
            pl.BlockSpec(memory_space=pl.ANY),
            pl.BlockSpec((rr, d), lambda i, j: (i * nj + j, 0)),
            pl.BlockSpec((rr, d), lambda i, j: (jnp.minimum(i + 1, ni - 1) * nj + j, 0)),
            pl.BlockSpec((1, d), lambda i, j: (0, 0)),
            pl.BlockSpec((1, d), lambda i, j: (0, 0)),
            pl.BlockSpec((d, tf), lambda i, j: (0, j)),
            pl.BlockSpec((tf, d), lambda i, j: (j, 0)),
        ] + [c[0] for c in cast],
        out_specs=[pl.BlockSpec(memory_space=pl.ANY)] + [c[1] for c in cast],
        out_shape=[jax.ShapeDtypeStruct((t, d), F32)] + [c[2] for c in cast],
        scratch_shapes=[
            pltpu.VMEM((tm, d), F32),
            pltpu.VMEM((2, tm, d), BF16),
            pltpu.VMEM((LOAD_SLOTS, pc, d), F32),
            pltpu.SemaphoreType.DMA((LOAD_SLOTS,)),
            pltpu.SemaphoreType.DMA((1,)),
        ],
        compiler_params=_params(("arbitrary", "arbitrary")),
        name="ffn",
    )(x, x, x, g, gf, wup, wdn, *[w for w, _ in cast_jobs])
    return outs[0], list(outs[1:])


def _pool_prompt_kernel(x_ref, xh_ref, g_ref, w_ref, sc_ref, o_ref, st_ref, p_ref, *, tt):
    t = pl.program_id(1)
    g = g_ref[...]
    d = x_ref.shape[2]
    gd = d // N_GROUPS
    for r in range(0, tt, ROW_CHUNK):
        p_ref[HALO + r:HALO + r + ROW_CHUNK, :] = _rms(x_ref[0, r:r + ROW_CHUNK, :], g)

    @pl.when(t == 0)
    def _():
        p_ref[0:HALO, :] = jnp.zeros((HALO, d), F32)

    @pl.when(t > 0)
    def _():
        p_ref[0:HALO, :] = _rms(xh_ref[0], g)

    pos = t * tt + lax.broadcasted_iota(jnp.int32, (tt, 1), 0)
    for gi, w in enumerate(POOL_WINDOWS):
        sl = slice(gi * gd, (gi + 1) * gd)
        u = p_ref[HALO:HALO + tt, sl]
        s = p_ref[0:HALO + tt, sl]
        k = 1
        while k < w:
            s = s + pltpu.roll(s, k, 0)
            k *= 2
        s = s[HALO:HALO + tt]
        cnt = jnp.minimum(pos + 1, w).astype(F32)
        diff = (s / cnt - u).astype(BF16)
        y = jnp.dot(diff, w_ref[gi], preferred_element_type=F32) * sc_ref[:, sl]
        o_ref[0, :, sl] = x_ref[0, :, sl] + y
    st_ref[0] = p_ref[HALO + tt - POOL_STATE_LEN:HALO + tt, :]


def _pool_prompt(x, g, w, layer, sc):
    b, s, d = x.shape
    gd = d // N_GROUPS
    tt = min(TILES["pool_tt"], s)
    hb = tt // HALO
    return pl.pallas_call(
        functools.partial(_pool_prompt_kernel, tt=tt),
        grid=(b, s // tt),
        in_specs=[
            pl.BlockSpec((1, tt, d), lambda i, t: (i, t, 0)),
            pl.BlockSpec((1, HALO, d), lambda i, t: (i, jnp.maximum(t * hb - 1, 0), 0)),
            pl.BlockSpec((1, d), lambda i, t: (0, 0)),
            pl.BlockSpec((None, N_GROUPS, gd, gd), lambda i, t: (layer, 0, 0, 0), pipeline_mode=pl.Buffered(1)),
            pl.BlockSpec((1, d), lambda i, t: (0, 0)),
        ],
        out_specs=[
            pl.BlockSpec((1, tt, d), lambda i, t: (i, t, 0)),
            pl.BlockSpec((1, POOL_STATE_LEN, d), lambda i, t: (i, 0, 0)),
        ],
        out_shape=[
            jax.ShapeDtypeStruct((b, s, d), F32),
            jax.ShapeDtypeStruct((b, POOL_STATE_LEN, d), F32),
        ],
        scratch_shapes=[pltpu.VMEM((HALO + tt, d), F32)],
        compiler_params=_params(("arbitrary", "arbitrary")),
        name="pool_prompt",
    )(x, x, g, w, sc)


def _pool_sample_kernel(x_ref, s_ref, g_ref, w_ref, sc_ref, *rest, bb, n_t, layer, first):
    o_ref, so_all_ref, u_ref, d_ref = rest[-4:]
    if first:
        so_ref = so_all_ref.at[layer]
        for other in range(so_all_ref.shape[0]):
            if other != layer:
                so_all_ref[other] = jnp.zeros(so_all_ref.shape[1:], F32)
    else:
        so_ref = so_all_ref
    d = u_ref.shape[2]
    gd = d // N_GROUPS
    g = g_ref[...]
    for t in range(n_t):
        u_ref[t] = _rms(x_ref[:, t * d:(t + 1) * d], g)

    def padded_row(idx, sl):
        if idx < POOL_STATE_LEN:
            return s_ref[:, idx, sl]
        return u_ref[idx - POOL_STATE_LEN, :, sl]

    for gi, w in enumerate(POOL_WINDOWS):
        sl = slice(gi * gd, (gi + 1) * gd)
        for t in range(n_t):
            s = padded_row(POOL_STATE_LEN + t, sl)
            for k in range(1, w):
                s = s + padded_row(POOL_STATE_LEN + t - k, sl)
            cnt = float(min(PAST_LEN + t + 1, w))
            d_ref[t * bb:(t + 1) * bb, sl] = (s / cnt - u_ref[t, :, sl]).astype(BF16)
        y = jnp.dot(d_ref[:, sl], w_ref[gi], preferred_element_type=F32) * sc_ref[:, sl]
        for t in range(n_t):
            cs = slice(t * d + sl.start, t * d + sl.stop)
            o_ref[:, cs] = x_ref[:, cs] + y[t * bb:(t + 1) * bb]

    keep = POOL_STATE_LEN - n_t
    for k in range(keep):
        so_ref[:, k, :] = s_ref[:, n_t + k, :]
    for t in range(n_t):
        so_ref[:, keep + t, :] = u_ref[t]


def _pool_sample(x, state, g, w, layer, sc, states_out):
    b, td = x.shape
    d = g.shape[1]
    gd = d // N_GROUPS
    n_t = td // d
    assert n_t <= POOL_STATE_LEN
    bb = min(TILES["pool_bb"], b)
    prev = [] if states_out is None else [states_out]
    n_layers = w.shape[0]
    if prev:
        so_spec = pl.BlockSpec((None, bb, POOL_STATE_LEN, d), lambda i: (layer, i, 0, 0))
    else:
        so_spec = pl.BlockSpec((n_layers, bb, POOL_STATE_LEN, d), lambda i: (0, i, 0, 0))
    return pl.pallas_call(
        functools.partial(_pool_sample_kernel, bb=bb, n_t=n_t, layer=layer, first=not prev),
        grid=(b // bb,),
        in_specs=[
            pl.BlockSpec((bb, td), lambda i: (i, 0)),
            pl.BlockSpec((None, bb, POOL_STATE_LEN, d), lambda i: (layer, i, 0, 0)),
            pl.BlockSpec((1, d), lambda i: (0, 0)),
            pl.BlockSpec((None, N_GROUPS, gd, gd), lambda i: (layer, 0, 0, 0), pipeline_mode=pl.Buffered(1)),
            pl.BlockSpec((1, d), lambda i: (0, 0)),
        ] + [pl.BlockSpec(memory_space=pl.ANY) for _ in prev],
        out_specs=[
            pl.BlockSpec((bb, td), lambda i: (i, 0)),
            so_spec,
        ],
        out_shape=[
            jax.ShapeDtypeStruct((b, td), F32),
            jax.ShapeDtypeStruct((n_layers, b, POOL_STATE_LEN, d), F32),
        ],
        input_output_aliases={5: 1} if prev else {},
        scratch_shapes=[pltpu.VMEM((n_t, bb, d), F32), pltpu.VMEM((n_t * bb, d), BF16)],
        compiler_params=_params(("arbitrary",)),
        name="pool_sample",
    )(x, state, g, w, sc, *prev)


def _conv_in_prompt_kernel(x_hbm, xn_ref, g_ref, wb_ref, wc_ref, wh_ref, cw_ref, o_ref, st_ref,
                           u_ref, v_ref, carry_ref, xbuf_ref, in_sem, *, tm, pc):
    t = pl.program_id(1)
    c = pl.program_id(2)
    tile = pl.program_id(0) * pl.num_programs(1) + t
    sub = F32_SUBLANES

    @pl.when((tile == 0) & (c == 0))
    def _():
        def src_rows(k):
            return x_hbm.at[0, pl.ds(pl.multiple_of(k * pc, pc), pc), :]

        _load_rows_normed(src_rows, u_ref.at[0], g_ref, xbuf_ref, in_sem, tm // pc, pc)

    u = u_ref[tile % 2]
    zc = jnp.dot(u, wc_ref[...], preferred_element_type=F32)
    zh = jnp.dot(u, wh_ref[...], preferred_element_type=F32)
    v_ref[sub:sub + tm, :] = zc * zh
    v_ref[0:sub, :] = jnp.where(t > 0, carry_ref[c], 0.0)

    conv = cw_ref[CONV_WIDTH - 1:CONV_WIDTH, :] * v_ref[sub:sub + tm, :]
    for k in range(CONV_WIDTH - 1):
        off = sub - (CONV_WIDTH - 1) + k
        conv = conv + cw_ref[k:k + 1, :] * v_ref[off:off + tm, :]
    zb = jnp.dot(u, wb_ref[...], preferred_element_type=F32)
    o_ref[0] = (zb * conv).astype(BF16)
    carry_ref[c] = v_ref[tm:tm + sub, :]

    rn = xn_ref.shape[1]
    rc = min(ROW_CHUNK, rn)
    for r in range(0, rn, rc):
        rows = pl.ds(pl.multiple_of(c * rn + r, rc), rc)
        u_ref[(tile + 1) % 2, rows, :] = _rms(xn_ref[0, r:r + rc, :], g_ref[...]).astype(BF16)

    @pl.when(t == pl.num_programs(1) - 1)
    def _():
        st_ref[0, c] = v_ref[sub + tm - CONV_STATE_LEN:sub + tm, :]


def _conv_in_prompt(x, g, w_in, layer, cw):
    b, s, d = x.shape
    tm, tn = min(TILES["conv_tm"], s), min(TILES["conv_tn"], d)
    pc = min(TILES["load_pc"], tm)
    nc = d // tn
    nt = s // tm
    rn = tm // nc
    assert rn % BF16_SUBLANES == 0, (tm, nc)

    def next_slab(i, t, c):
        nxt = jnp.minimum(i * nt + t + 1, b * nt - 1)
        return (nxt // nt, (nxt % nt) * nc + c, 0)

    gated, st = pl.pallas_call(
        functools.partial(_conv_in_prompt_kernel, tm=tm, pc=pc),
        grid=(b, nt, nc),
        in_specs=[
            pl.BlockSpec(memory_space=pl.ANY),
            pl.BlockSpec((1, rn, d), next_slab),
            pl.BlockSpec((1, d), lambda i, t, c: (0, 0)),
            pl.BlockSpec((d, tn), lambda i, t, c: (0, c)),
            pl.BlockSpec((d, tn), lambda i, t, c: (0, nc + c)),
            pl.BlockSpec((d, tn), lambda i, t, c: (0, 2 * nc + c)),
            pl.BlockSpec((None, CONV_WIDTH, tn), lambda i, t, c: (layer, 0, c)),
        ],
        out_specs=[
            pl.BlockSpec((1, tm, tn), lambda i, t, c: (i, t, c)),
            pl.BlockSpec((1, nc, CONV_STATE_LEN, tn), lambda i, t, c: (i, 0, 0, 0)),
        ],
        out_shape=[
            jax.ShapeDtypeStruct((b, s, d), BF16),
            jax.ShapeDtypeStruct((b, nc, CONV_STATE_LEN, tn), F32),
        ],
        scratch_shapes=[
            pltpu.VMEM((2, tm, d), BF16),
            pltpu.VMEM((F32_SUBLANES + tm, tn), F32),
            pltpu.VMEM((nc, F32_SUBLANES, tn), F32),
            pltpu.VMEM((LOAD_SLOTS, pc, d), F32),
            pltpu.SemaphoreType.DMA((LOAD_SLOTS,)),
        ],
        compiler_params=_params(("arbitrary", "arbitrary", "arbitrary")),
        name="conv_in_prompt",
    )(x, x, g, w_in, w_in, w_in, cw)
    return gated, st.transpose(0, 2, 1, 3).reshape(b, CONV_STATE_LEN, d)


def _conv_in_sample_kernel(x_ref, s0_ref, s1_ref, g_ref, wb_ref, wc_ref, wh_ref, cw_ref,
                           o_ref, st0_ref, st1_ref, u_ref, *, b, n_t):
    c = pl.program_id(0)
    d = u_ref.shape[1]

    @pl.when(c == 0)
    def _():
        for t in range(n_t):
            u_ref[t * b:(t + 1) * b, :] = _rms(x_ref[:, t * d:(t + 1) * d], g_ref[...]).astype(BF16)

    u = u_ref[...]
    zc = jnp.dot(u, wc_ref[...], preferred_element_type=F32)
    zh = jnp.dot(u, wh_ref[...], preferred_element_type=F32)
    zb = jnp.dot(u, wb_ref[...], preferred_element_type=F32)
    v = zc * zh
    rows = [s0_ref[...], s1_ref[...]] + [v[t * b:(t + 1) * b] for t in range(n_t)]
    for t in range(n_t):
        conv = cw_ref[0:1, :] * rows[t]
        for k in range(1, CONV_WIDTH):
            conv = conv + cw_ref[k:k + 1, :] * rows[t + k]
        o_ref[t * b:(t + 1) * b, :] = (zb[t * b:(t + 1) * b] * conv).astype(BF16)
    st0_ref[...] = rows[n_t]
    st1_ref[...] = rows[n_t + 1]


def _conv_in_sample(x, state, g, w_in, layer, cw):
    b, td = x.shape
    d = g.shape[1]
    n_t = td // d
    tn = min(TILES["conv_tn"], d)
    nc = d // tn
    assert CONV_STATE_LEN == 2
    return pl.pallas_call(
        functools.partial(_conv_in_sample_kernel, b=b, n_t=n_t),
        grid=(nc,),
        in_specs=[
            pl.BlockSpec((b, td), lambda c: (0, 0)),
            pl.BlockSpec((b, tn), lambda c: (0, c)),
            pl.BlockSpec((b, tn), lambda c: (0, nc + c)),
            pl.BlockSpec((1, d), lambda c: (0, 0)),
            pl.BlockSpec((d, tn), lambda c: (0, c)),
            pl.BlockSpec((d, tn), lambda c: (0, nc + c)),
            pl.BlockSpec((d, tn), lambda c: (0, 2 * nc + c)),
            pl.BlockSpec((None, CONV_WIDTH, tn), lambda c: (layer, 0, c)),
        ],
        out_specs=[
            pl.BlockSpec((n_t * b, tn), lambda c: (0, c)),
            pl.BlockSpec((b, tn), lambda c: (0, c)),
            pl.BlockSpec((b, tn), lambda c: (0, c)),
        ],
        out_shape=[
            jax.ShapeDtypeStruct((n_t * b, d), BF16),
            jax.ShapeDtypeStruct((b, d), F32),
            jax.ShapeDtypeStruct((b, d), F32),
        ],
        scratch_shapes=[pltpu.VMEM((n_t * b, d), BF16)],
        compiler_params=_params(("arbitrary",)),
        name="conv_in_sample",
    )(x, state, state, g, w_in, w_in, w_in, cw)


def _proj_res_kernel(a_ref, w_ref, x_ref, o_ref):
    o_ref[...] = x_ref[...] + jnp.dot(a_ref[...], w_ref[...], preferred_element_type=F32)


def _conv_out_prompt(a, w, x):
    t, d = x.shape
    tm, tn = min(TILES["proj_tm"], t), min(TILES["proj_tn"], d)
    return pl.pallas_call(
        _proj_res_kernel,
        grid=(t // tm, d // tn),
        in_specs=[
            pl.BlockSpec((tm, d), lambda i, n: (i, 0)),
            pl.BlockSpec((d, tn), lambda i, n: (0, n)),
            pl.BlockSpec((tm, tn), lambda i, n: (i, n)),
        ],
        out_specs=pl.BlockSpec((tm, tn), lambda i, n: (i, n)),
        out_shape=jax.ShapeDtypeStruct((t, d), F32),
        compiler_params=_params(("arbitrary", "arbitrary")),
        name="conv_out_prompt",
    )(a, w, x)


def _conv_out_sample(a, w, x):
    b, td = x.shape
    d = w.shape[0]
    n_t = td // d
    tn = min(TILES["proj_tn"], d)
    nn = d // tn
    return pl.pallas_call(
        _proj_res_kernel,
        grid=(n_t, nn),
        in_specs=[
            pl.BlockSpec((b, d), lambda t, n: (t, 0)),
            pl.BlockSpec((d, tn), lambda t, n: (0, n)),
            pl.BlockSpec((b, tn), lambda t, n: (0, t * nn + n)),
        ],
        out_specs=pl.BlockSpec((b, tn), lambda t, n: (0, t * nn + n)),
        out_shape=jax.ShapeDtypeStruct((b, td), F32),
        compiler_params=_params(("arbitrary", "arbitrary")),
        name="conv_out_sample",
    )(a, w, x)


def kernel(x_prompt, x_sample, state_pool, state_conv, norm_mix, norm_mlp, norm_final,
           w_pool, pool_scale, w_conv_in, conv_w, w_conv_out, w_up, w_down):
    bp, sp, d = x_prompt.shape
    bs, ss, _ = x_sample.shape

    w_pool_b = w_pool.astype(BF16)
    wup_b, wdn_b = w_up[0].astype(BF16), w_down[0].astype(BF16)
    wci_b = wco_b = None
    g_fin = norm_final.reshape(1, d)

    xp = x_prompt
    xs = x_sample.reshape(bs, ss * d)
    pool_p, conv_p, conv_s = [], [], []
    pool_s = None
    for i in range(DEPTH):
        j = i // 2
        last = i == DEPTH - 1
        g_mix = norm_mix[i].reshape(1, d)
        if i % 2 == 0:
            sc = pool_scale[j].reshape(1, d)
            xp, st = _pool_prompt(xp, g_mix, w_pool_b, j, sc)
            pool_p.append(st)
            xs, pool_s = _pool_sample(xs, state_pool, g_mix, w_pool_b, j, sc, pool_s)
        else:
            a, st = _conv_in_prompt(xp, g_mix, wci_b, j, conv_w)
            conv_p.append(st)
            xp = _conv_out_prompt(a.reshape(bp * sp, d), wco_b, xp.reshape(bp * sp, d)).reshape(bp, sp, d)
            a, st0, st1 = _conv_in_sample(xs, state_conv[j].reshape(bs, CONV_STATE_LEN * d), g_mix,
                                          wci_b, j, conv_w)
            conv_s.append(jnp.stack([st0, st1], axis=1))
            xs = _conv_out_sample(a, wco_b, xs)
        g_mlp = norm_mlp[i].reshape(1, d)
        jobs = []
        if not last:
            if (i + 1) % 2 == 1:
                jobs += [(w_conv_in, (i + 1) // 2), (w_conv_out, (i + 1) // 2)]
            jobs += [(w_up, i + 1), (w_down, i + 1)]
        xp, cast = _ffn(xp.reshape(bp * sp, d), g_mlp, g_fin, wup_b, wdn_b, jobs, final=last)
        xp = xp.reshape(bp, sp, d)
        xs, _ = _ffn(xs.reshape(bs * ss, d), g_mlp, g_fin, wup_b, wdn_b, final=last)
        xs = xs.reshape(bs, ss * d)
        if not last:
            if (i + 1) % 2 == 1:
                wci_b, wco_b = cast[0], cast[1]
            wup_b, wdn_b = cast[-2], cast[-1]

    y_sample = xs.reshape(bs, ss, d)
    return (xp, y_sample, jnp.stack(pool_p), pool_s, jnp.stack(conv_p), jnp.stack(conv_s))
```

```python
import functools

import jax
import jax.numpy as jnp
from jax import lax
from jax.experimental import pallas as pl
from jax.experimental.pallas import tpu as pltpu

DEPTH = 4
PAST_LEN = 16384
POOL_WINDOWS = (2, 4, 8, 16)
N_GROUPS = len(POOL_WINDOWS)
POOL_STATE_LEN = max(POOL_WINDOWS) - 1
CONV_WIDTH = 3
CONV_STATE_LEN = CONV_WIDTH - 1
RMS_EPS = 1e-6

F32 = jnp.float32
BF16 = jnp.bfloat16

V7X_VMEM_BYTES = 64 * 1024 * 1024
VMEM_LIMIT = V7X_VMEM_BYTES - 1024 * 1024
F32_SUBLANES = 8
BF16_SUBLANES = 16
HALO = 2 * F32_SUBLANES
assert POOL_STATE_LEN <= HALO and all(w & (w - 1) == 0 for w in POOL_WINDOWS)

TILES = dict(
    ffn_tm=1024, ffn_tf=512, ffn_tn=512, load_pc=32,
    pool_tt=512, pool_bb=16,
    conv_tm=1024, conv_tn=512,
    proj_tm=1024, proj_tn=1024,
)


def _params(sem):
    return pltpu.CompilerParams(dimension_semantics=sem, vmem_limit_bytes=VMEM_LIMIT)


def _rms(x, g):
    inv = lax.rsqrt(jnp.mean(x * x, axis=-1, keepdims=True) + RMS_EPS)
    return (x * inv) * g


ROW_CHUNK = 2 * BF16_SUBLANES


def _for_row_chunks(n_rows, fn):
    rc = min(ROW_CHUNK, n_rows)

    def body(r, carry):
        fn(pl.ds(pl.multiple_of(r * rc, rc), rc))
        return carry

    lax.fori_loop(0, n_rows // rc, body, 0)


LOAD_SLOTS = 3


def _load_rows_normed(src_rows, u_ref, g_ref, xbuf_ref, sem, n_chunks, pc):
    def copy(c):
        slot = c % LOAD_SLOTS
        return pltpu.make_async_copy(src_rows(c), xbuf_ref.at[slot], sem.at[slot])

    for c in range(min(LOAD_SLOTS - 1, n_chunks)):
        copy(c).start()

    def chunk_body(c, carry):
        @pl.when(c + LOAD_SLOTS - 1 < n_chunks)
        def _():
            copy(c + LOAD_SLOTS - 1).start()

        copy(c).wait()
        slot = c % LOAD_SLOTS
        for r in range(0, pc, ROW_CHUNK):
            rows = pl.ds(pl.multiple_of(c * pc + r, ROW_CHUNK), ROW_CHUNK)
            u_ref[rows, :] = _rms(xbuf_ref[slot, r:r + ROW_CHUNK, :], g_ref[...]).astype(BF16)
        return carry

    lax.fori_loop(0, n_chunks, chunk_body, 0)


def _ffn_kernel(*refs, n_cast, tm, tn, pc, final):
    x_hbm, xr_ref, xn_ref, g_ref, gf_ref, wup_ref, wdn_ref = refs[:7]
    cast_in = refs[7:7 + n_cast]
    o_hbm = refs[7 + n_cast]
    cast_out = refs[8 + n_cast:8 + 2 * n_cast]
    acc_ref, u_ref, xbuf_ref, in_sem, out_sem = refs[8 + 2 * n_cast:]
    i, j = pl.program_id(0), pl.program_id(1)
    ni, nj = pl.num_programs(0), pl.num_programs(1)
    rr = xr_ref.shape[0]

    def tile_writeback(tile):
        rows = pl.ds(pl.multiple_of(tile * tm, tm), tm)
        return pltpu.make_async_copy(acc_ref, o_hbm.at[rows, :], out_sem.at[0])

    @pl.when(j == 0)
    def _():
        @pl.when(i == 0)
        def _():
            def src_rows(c):
                return x_hbm.at[pl.ds(pl.multiple_of(c * pc, pc), pc), :]

            _load_rows_normed(src_rows, u_ref.at[0], g_ref, xbuf_ref, in_sem, tm // pc, pc)

        @pl.when(i > 0)
        def _():
            tile_writeback(i - 1).wait()

        def zero_rows(rows):
            acc_ref[rows, :] = jnp.zeros((rows.size, acc_ref.shape[1]), F32)

        _for_row_chunks(tm, zero_rows)

    h = jnp.dot(u_ref[i % 2], wup_ref[...], preferred_element_type=F32)
    h = jnp.maximum(h, 0.0)
    h = (h * h).astype(BF16)
    casts = list(zip(cast_in, cast_out))
    for k, n in enumerate(range(0, acc_ref.shape[1], tn)):
        acc_ref[:, n:n + tn] += jnp.dot(h, wdn_ref[:, n:n + tn], preferred_element_type=F32)
        if k < len(casts):
            casts[k][1][...] = casts[k][0][...].astype(BF16)
    for src, dst in casts[acc_ref.shape[1] // tn:]:
        dst[...] = src[...].astype(BF16)
    slab = pl.ds(pl.multiple_of(j * rr, rr), rr)
    acc_ref[slab, :] += xr_ref[...]
    u_ref[(i + 1) % 2, slab, :] = _rms(xn_ref[...], g_ref[...]).astype(BF16)

    @pl.when(j == nj - 1)
    def _():
        if final:
            def final_rows(rows):
                acc_ref[rows, :] = _rms(acc_ref[rows, :], gf_ref[...])

            _for_row_chunks(tm, final_rows)
        tile_writeback(i).start()

        @pl.when(i == ni - 1)
        def _():
            tile_writeback(i).wait()


def _cast_specs(w, layer, n_steps, nj):
    _, r, c = w.shape
    reps = 1
    while (r * reps) % n_steps or (r * reps // n_steps) % BF16_SUBLANES:
        reps *= 2
        assert reps <= n_steps, (w.shape, n_steps)
    rb = r * reps // n_steps
    in_spec = pl.BlockSpec((None, rb, c), lambda i, j: (layer, (i * nj + j) // reps, 0))
    out_spec = pl.BlockSpec((rb, c), lambda i, j: ((i * nj + j) // reps, 0))
    return in_spec, out_spec, jax.ShapeDtypeStruct((r, c), BF16)


def _ffn(x, g, gf, wup, wdn, cast_jobs=(), *, final):
    t, d = x.shape
    f = wup.shape[1]
    tm, tf, tn = min(TILES["ffn_tm"], t), min(TILES["ffn_tf"], f), min(TILES["ffn_tn"], d)
    pc = min(TILES["load_pc"], tm)
    ni, nj = t // tm, f // tf
    rr = tm // nj
    assert rr % BF16_SUBLANES == 0 and pc % ROW_CHUNK == 0, (tm, nj, pc)
    cast = [_cast_specs(w, layer, ni * nj, nj) for w, layer in cast_jobs]
    outs = pl.pallas_call(
        functools.partial(_ffn_kernel, n_cast=len(cast), tm=tm, tn=tn, pc=pc, final=final),
        grid=(ni, nj),
        in_specs=[
            pl.BlockSpec(memory_space=pl.ANY),
            pl.BlockSpec((rr, d), lambda i, j: (i * nj + j, 0)),
            pl.BlockSpec((rr, d), lambda i, j: (jnp.minimum(i + 1, ni - 1) * nj + j, 0)),
            pl.BlockSpec((1, d), lambda i, j: (0, 0)),
            pl.BlockSpec((1, d), lambda i, j: (0, 0)),
            pl.BlockSpec((d, tf), lambda i, j: (0, j)),
            pl.BlockSpec((tf, d), lambda i, j: (j, 0)),
        ] + [c[0] for c in cast],
        out_specs=[pl.BlockSpec(memory_space=pl.ANY)] + [c[1] for c in cast],
        out_shape=[jax.ShapeDtypeStruct((t, d), F32)] + [c[2] for c in cast],
        scratch_shapes=[
            pltpu.VMEM((tm, d), F32),
            pltpu.VMEM((2, tm, d), BF16),
            pltpu.VMEM((LOAD_SLOTS, pc, d), F32),
            pltpu.SemaphoreType.DMA((LOAD_SLOTS,)),
            pltpu.SemaphoreType.DMA((1,)),
        ],
        compiler_params=_params(("arbitrary", "arbitrary")),
        name="ffn",
    )(x, x, x, g, gf, wup, wdn, *[w for w, _ in cast_jobs])
    return outs[0], list(outs[1:])


def _pool_prompt_kernel(x_ref, xh_ref, g_ref, w_ref, sc_ref, o_ref, st_ref, p_ref, *, tt):
    t = pl.program_id(1)
    g = g_ref[...]
    d = x_ref.shape[2]
    gd = d // N_GROUPS
    for r in range(0, tt, ROW_CHUNK):
        p_ref[HALO + r:HALO + r + ROW_CHUNK, :] = _rms(x_ref[0, r:r + ROW_CHUNK, :], g)

    @pl.when(t == 0)
    def _():
        p_ref[0:HALO, :] = jnp.zeros((HALO, d), F32)

    @pl.when(t > 0)
    def _():
        p_ref[0:HALO, :] = _rms(xh_ref[0], g)

    pos = t * tt + lax.broadcasted_iota(jnp.int32, (tt, 1), 0)
    for gi, w in enumerate(POOL_WINDOWS):
        sl = slice(gi * gd, (gi + 1) * gd)
        u = p_ref[HALO:HALO + tt, sl]
        s = p_ref[0:HALO + tt, sl]
        k = 1
        while k < w:
            s = s + pltpu.roll(s, k, 0)
            k *= 2
        s = s[HALO:HALO + tt]
        cnt = jnp.minimum(pos + 1, w).astype(F32)
        diff = (s / cnt - u).astype(BF16)
        y = jnp.dot(diff, w_ref[gi], preferred_element_type=F32) * sc_ref[:, sl]
        o_ref[0, :, sl] = x_ref[0, :, sl] + y
    st_ref[0] = p_ref[HALO + tt - POOL_STATE_LEN:HALO + tt, :]


def _pool_prompt(x, g, w, layer, sc):
    b, s, d = x.shape
    gd = d // N_GROUPS
    tt = min(TILES["pool_tt"], s)
    hb = tt // HALO
    return pl.pallas_call(
        functools.partial(_pool_prompt_kernel, tt=tt),
        grid=(b, s // tt),
        in_specs=[
            pl.BlockSpec((1, tt, d), lambda i, t: (i, t, 0)),
            pl.BlockSpec((1, HALO, d), lambda i, t: (i, jnp.maximum(t * hb - 1, 0), 0)),
            pl.BlockSpec((1, d), lambda i, t: (0, 0)),
            pl.BlockSpec((None, N_GROUPS, gd, gd), lambda i, t: (layer, 0, 0, 0), pipeline_mode=pl.Buffered(1)),
            pl.BlockSpec((1, d), lambda i, t: (0, 0)),
        ],
        out_specs=[
            pl.BlockSpec((1, tt, d), lambda i, t: (i, t, 0)),
            pl.BlockSpec((1, POOL_STATE_LEN, d), lambda i, t: (i, 0, 0)),
        ],
        out_shape=[
            jax.ShapeDtypeStruct((b, s, d), F32),
            jax.ShapeDtypeStruct((b, POOL_STATE_LEN, d), F32),
        ],
        scratch_shapes=[pltpu.VMEM((HALO + tt, d), F32)],
        compiler_params=_params(("arbitrary", "arbitrary")),
        name="pool_prompt",
    )(x, x, g, w, sc)


def _pool_sample_kernel(x_ref, s_ref, g_ref, w_ref, sc_ref, *rest, bb, n_t, layer, first):
    o_ref, so_all_ref, u_ref, d_ref = rest[-4:]
    if first:
        so_ref = so_all_ref.at[layer]
        for other in range(so_all_ref.shape[0]):
            if other != layer:
                so_all_ref[other] = jnp.zeros(so_all_ref.shape[1:], F32)
    else:
        so_ref = so_all_ref
    d = u_ref.shape[2]
    gd = d // N_GROUPS
    g = g_ref[...]
    for t in range(n_t):
        u_ref[t] = _rms(x_ref[t], g)

    def padded_row(idx, sl):
        if idx < POOL_STATE_LEN:
            return s_ref[:, idx, sl]
        return u_ref[idx - POOL_STATE_LEN, :, sl]

    for gi, w in enumerate(POOL_WINDOWS):
        sl = slice(gi * gd, (gi + 1) * gd)
        for t in range(n_t):
            s = padded_row(POOL_STATE_LEN + t, sl)
            for k in range(1, w):
                s = s + padded_row(POOL_STATE_LEN + t - k, sl)
            cnt = float(min(PAST_LEN + t + 1, w))
            d_ref[t * bb:(t + 1) * bb, sl] = (s / cnt - u_ref[t, :, sl]).astype(BF16)
        y = jnp.dot(d_ref[:, sl], w_ref[gi], preferred_element_type=F32) * sc_ref[:, sl]
        for t in range(n_t):
            o_ref[t, :, sl] = x_ref[t, :, sl] + y[t * bb:(t + 1) * bb]

    keep = POOL_STATE_LEN - n_t
    for k in range(keep):
        so_ref[:, k, :] = s_ref[:, n_t + k, :]
    for t in range(n_t):
        so_ref[:, keep + t, :] = u_ref[t]


def _pool_sample(x, state, g, w, layer, sc, states_out):
    n_t, b, d = x.shape
    gd = d // N_GROUPS
    assert n_t <= POOL_STATE_LEN
    bb = min(TILES["pool_bb"], b)
    prev = [] if states_out is None else [states_out]
    n_layers = w.shape[0]
    if prev:
        so_spec = pl.BlockSpec((None, bb, POOL_STATE_LEN, d), lambda i: (layer, i, 0, 0))
    else:
        so_spec = pl.BlockSpec((n_layers, bb, POOL_STATE_LEN, d), lambda i: (0, i, 0, 0))
    return pl.pallas_call(
        functools.partial(_pool_sample_kernel, bb=bb, n_t=n_t, layer=layer, first=not prev),
        grid=(b // bb,),
        in_specs=[
            pl.BlockSpec((n_t, bb, d), lambda i: (0, i, 0)),
            pl.BlockSpec((None, bb, POOL_STATE_LEN, d), lambda i: (layer, i, 0, 0)),
            pl.BlockSpec((1, d), lambda i: (0, 0)),
            pl.BlockSpec((None, N_GROUPS, gd, gd), lambda i: (layer, 0, 0, 0), pipeline_mode=pl.Buffered(1)),
            pl.BlockSpec((1, d), lambda i: (0, 0)),
        ] + [pl.BlockSpec(memory_space=pl.ANY) for _ in prev],
        out_specs=[
            pl.BlockSpec((n_t, bb, d), lambda i: (0, i, 0)),
            so_spec,
        ],
        out_shape=[
            jax.ShapeDtypeStruct((n_t, b, d), F32),
            jax.ShapeDtypeStruct((n_layers, b, POOL_STATE_LEN, d), F32),
        ],
        input_output_aliases={5: 1} if prev else {},
        scratch_shapes=[pltpu.VMEM((n_t, bb, d), F32), pltpu.VMEM((n_t * bb, d), BF16)],
        compiler_params=_params(("arbitrary",)),
        name="pool_sample",
    )(x, state, g, w, sc, *prev)


def _conv_in_prompt_kernel(x_hbm, xn_ref, g_ref, wb_ref, wc_ref, wh_ref, cw_ref, o_ref, st_ref,
                           u_ref, v_ref, carry_ref, xbuf_ref, in_sem, *, tm, pc):
    t = pl.program_id(1)
    c = pl.program_id(2)
    tile = pl.program_id(0) * pl.num_programs(1) + t
    sub = F32_SUBLANES

    @pl.when((tile == 0) & (c == 0))
    def _():
        def src_rows(k):
            return x_hbm.at[0, pl.ds(pl.multiple_of(k * pc, pc), pc), :]

        _load_rows_normed(src_rows, u_ref.at[0], g_ref, xbuf_ref, in_sem, tm // pc, pc)

    u = u_ref[tile % 2]
    zc = jnp.dot(u, wc_ref[...], preferred_element_type=F32)
    zh = jnp.dot(u, wh_ref[...], preferred_element_type=F32)
    v_ref[sub:sub + tm, :] = zc * zh
    v_ref[0:sub, :] = jnp.where(t > 0, carry_ref[c], 0.0)

    conv = cw_ref[CONV_WIDTH - 1:CONV_WIDTH, :] * v_ref[sub:sub + tm, :]
    for k in range(CONV_WIDTH - 1):
        off = sub - (CONV_WIDTH - 1) + k
        conv = conv + cw_ref[k:k + 1, :] * v_ref[off:off + tm, :]
    zb = jnp.dot(u, wb_ref[...], preferred_element_type=F32)
    o_ref[0] = (zb * conv).astype(BF16)
    carry_ref[c] = v_ref[tm:tm + sub, :]

    rn = xn_ref.shape[1]
    rc = min(ROW_CHUNK, rn)
    for r in range(0, rn, rc):
        rows = pl.ds(pl.multiple_of(c * rn + r, rc), rc)
        u_ref[(tile + 1) % 2, rows, :] = _rms(xn_ref[0, r:r + rc, :], g_ref[...]).astype(BF16)

    @pl.when(t == pl.num_programs(1) - 1)
    def _():
        st_ref[0, c] = v_ref[sub + tm - CONV_STATE_LEN:sub + tm, :]


def _conv_in_prompt(x, g, w_in, layer, cw):
    b, s, d = x.shape
    tm, tn = min(TILES["conv_tm"], s), min(TILES["conv_tn"], d)
    pc = min(TILES["load_pc"], tm)
    nc = d // tn
    nt = s // tm
    rn = tm // nc
    assert rn % BF16_SUBLANES == 0, (tm, nc)

    def next_slab(i, t, c):
        nxt = jnp.minimum(i * nt + t + 1, b * nt - 1)
        return (nxt // nt, (nxt % nt) * nc + c, 0)

    gated, st = pl.pallas_call(
        functools.partial(_conv_in_prompt_kernel, tm=tm, pc=pc),
        grid=(b, nt, nc),
        in_specs=[
            pl.BlockSpec(memory_space=pl.ANY),
            pl.BlockSpec((1, rn, d), next_slab),
            pl.BlockSpec((1, d), lambda i, t, c: (0, 0)),
            pl.BlockSpec((d, tn), lambda i, t, c: (0, c)),
            pl.BlockSpec((d, tn), lambda i, t, c: (0, nc + c)),
            pl.BlockSpec((d, tn), lambda i, t, c: (0, 2 * nc + c)),
            pl.BlockSpec((None, CONV_WIDTH, tn), lambda i, t, c: (layer, 0, c)),
        ],
        out_specs=[
            pl.BlockSpec((1, tm, tn), lambda i, t, c: (i, t, c)),
            pl.BlockSpec((1, nc, CONV_STATE_LEN, tn), lambda i, t, c: (i, 0, 0, 0)),
        ],
        out_shape=[
            jax.ShapeDtypeStruct((b, s, d), BF16),
            jax.ShapeDtypeStruct((b, nc, CONV_STATE_LEN, tn), F32),
        ],
        scratch_shapes=[
            pltpu.VMEM((2, tm, d), BF16),
            pltpu.VMEM((F32_SUBLANES + tm, tn), F32),
            pltpu.VMEM((nc, F32_SUBLANES, tn), F32),
            pltpu.VMEM((LOAD_SLOTS, pc, d), F32),
            pltpu.SemaphoreType.DMA((LOAD_SLOTS,)),
        ],
        compiler_params=_params(("arbitrary", "arbitrary", "arbitrary")),
        name="conv_in_prompt",
    )(x, x, g, w_in, w_in, w_in, cw)
    return gated, st.transpose(0, 2, 1, 3).reshape(b, CONV_STATE_LEN, d)


def _conv_in_sample_kernel(x_ref, s0_ref, s1_ref, g_ref, wb_ref, wc_ref, wh_ref, cw_ref,
                           o_ref, st0_ref, st1_ref, u_ref, *, b, n_t):
    c = pl.program_id(0)

    @pl.when(c == 0)
    def _():
        for t in range(n_t):
            u_ref[t * b:(t + 1) * b, :] = _rms(x_ref[t], g_ref[...]).astype(BF16)

    u = u_ref[...]
    zc = jnp.dot(u, wc_ref[...], preferred_element_type=F32)
    zh = jnp.dot(u, wh_ref[...], preferred_element_type=F32)
    zb = jnp.dot(u, wb_ref[...], preferred_element_type=F32)
    v = zc * zh
    rows = [s0_ref[...], s1_ref[...]] + [v[t * b:(t + 1) * b] for t in range(n_t)]
    for t in range(n_t):
        conv = cw_ref[0:1, :] * rows[t]
        for k in range(1, CONV_WIDTH):
            conv = conv + cw_ref[k:k + 1, :] * rows[t + k]
        o_ref[t * b:(t + 1) * b, :] = (zb[t * b:(t + 1) * b] * conv).astype(BF16)
    st0_ref[...] = rows[n_t]
    st1_ref[...] = rows[n_t + 1]


def _conv_in_sample(x, state, g, w_in, layer, cw):
    n_t, b, d = x.shape
    tn = min(TILES["conv_tn"], d)
    nc = d // tn
    assert CONV_STATE_LEN == 2
    return pl.pallas_call(
        functools.partial(_conv_in_sample_kernel, b=b, n_t=n_t),
        grid=(nc,),
        in_specs=[
            pl.BlockSpec((n_t, b, d), lambda c: (0, 0, 0)),
            pl.BlockSpec((b, tn), lambda c: (0, c)),
            pl.BlockSpec((b, tn), lambda c: (0, nc + c)),
            pl.BlockSpec((1, d), lambda c: (0, 0)),
            pl.BlockSpec((d, tn), lambda c: (0, c)),
            pl.BlockSpec((d, tn), lambda c: (0, nc + c)),
            pl.BlockSpec((d, tn), lambda c: (0, 2 * nc + c)),
            pl.BlockSpec((None, CONV_WIDTH, tn), lambda c: (layer, 0, c)),
        ],
        out_specs=[
            pl.BlockSpec((n_t * b, tn), lambda c: (0, c)),
            pl.BlockSpec((b, tn), lambda c: (0, c)),
            pl.BlockSpec((b, tn), lambda c: (0, c)),
        ],
        out_shape=[
            jax.ShapeDtypeStruct((n_t * b, d), BF16),
            jax.ShapeDtypeStruct((b, d), F32),
            jax.ShapeDtypeStruct((b, d), F32),
        ],
        scratch_shapes=[pltpu.VMEM((n_t * b, d), BF16)],
        compiler_params=_params(("arbitrary",)),
        name="conv_in_sample",
    )(x, state, state, g, w_in, w_in, w_in, cw)


def _proj_res_kernel(a_ref, w_ref, x_ref, o_ref):
    o_ref[...] = x_ref[...] + jnp.dot(a_ref[...], w_ref[...], preferred_element_type=F32)


def _conv_out(a, w, x):
    t, d = x.shape
    tm, tn = min(TILES["proj_tm"], t), min(TILES["proj_tn"], d)
    return pl.pallas_call(
        _proj_res_kernel,
        grid=(t // tm, d // tn),
        in_specs=[
            pl.BlockSpec((tm, d), lambda i, n: (i, 0)),
            pl.BlockSpec((d, tn), lambda i, n: (0, n)),
            pl.BlockSpec((tm, tn), lambda i, n: (i, n)),
        ],
        out_specs=pl.BlockSpec((tm, tn), lambda i, n: (i, n)),
        out_shape=jax.ShapeDtypeStruct((t, d), F32),
        compiler_params=_params(("arbitrary", "arbitrary")),
        name="conv_out",
    )(a, w, x)


def kernel(x_prompt, x_sample, state_pool, state_conv, norm_mix, norm_mlp, norm_final,
           w_pool, pool_scale, w_conv_in, conv_w, w_conv_out, w_up, w_down):
    bp, sp, d = x_prompt.shape
    bs, ss, _ = x_sample.shape

    w_pool_b = w_pool.astype(BF16)
    wup_b, wdn_b = w_up[0].astype(BF16), w_down[0].astype(BF16)
    wci_b = wco_b = None
    g_fin = norm_final.reshape(1, d)

    xp = x_prompt
    xs = x_sample.transpose(1, 0, 2)
    pool_p, conv_p, conv_s = [], [], []
    pool_s = None
    for i in range(DEPTH):
        j = i // 2
        last = i == DEPTH - 1
        g_mix = norm_mix[i].reshape(1, d)
        if i % 2 == 0:
            sc = pool_scale[j].reshape(1, d)
            xp, st = _pool_prompt(xp, g_mix, w_pool_b, j, sc)
            pool_p.append(st)
            xs, pool_s = _pool_sample(xs, state_pool, g_mix, w_pool_b, j, sc, pool_s)
        else:
            a, st = _conv_in_prompt(xp, g_mix, wci_b, j, conv_w)
            conv_p.append(st)
            xp = _conv_out(a.reshape(bp * sp, d), wco_b, xp.reshape(bp * sp, d)).reshape(bp, sp, d)
            a, st0, st1 = _conv_in_sample(xs, state_conv[j].reshape(bs, CONV_STATE_LEN * d), g_mix,
                                          wci_b, j, conv_w)
            conv_s.append(jnp.stack([st0, st1], axis=1))
            xs = _conv_out(a, wco_b, xs.reshape(ss * bs, d)).reshape(ss, bs, d)
        g_mlp = norm_mlp[i].reshape(1, d)
        jobs = []
        if not last:
            if (i + 1) % 2 == 1:
                jobs += [(w_conv_in, (i + 1) // 2), (w_conv_out, (i + 1) // 2)]
            jobs += [(w_up, i + 1), (w_down, i + 1)]
        xp, cast = _ffn(xp.reshape(bp * sp, d), g_mlp, g_fin, wup_b, wdn_b, jobs, final=last)
        xp = xp.reshape(bp, sp, d)
        xs, _ = _ffn(xs.reshape(ss * bs, d), g_mlp, g_fin, wup_b, wdn_b, final=last)
        xs = xs.reshape(ss, bs, d)
        if not last:
            if (i + 1) % 2 == 1:
                wci_b, wco_b = cast[0], cast[1]
            wup_b, wdn_b = cast[-2], cast[-1]

    y_sample = xs.transpose(1, 0, 2)
    return (xp, y_sample, jnp.stack(pool_p), pool_s, jnp.stack(conv_p), jnp.stack(conv_s))
```

```python
import functools

import jax
import jax.numpy as jnp
from jax import lax
from jax.experimental import pallas as pl
from jax.experimental.pallas import tpu as pltpu

DEPTH = 4
PAST_LEN = 16384
POOL_WINDOWS = (2, 4, 8, 16)
N_GROUPS = len(POOL_WINDOWS)
POOL_STATE_LEN = max(POOL_WINDOWS) - 1
CONV_WIDTH = 3
CONV_STATE_LEN = CONV_WIDTH - 1
RMS_EPS = 1e-6

F32 = jnp.float32
BF16 = jnp.bfloat16

V7X_VMEM_BYTES = 64 * 1024 * 1024
VMEM_LIMIT = V7X_VMEM_BYTES - 1024 * 1024
F32_SUBLANES = 8
BF16_SUBLANES = 16
HALO = 2 * F32_SUBLANES
assert POOL_STATE_LEN <= HALO and all(w & (w - 1) == 0 for w in POOL_WINDOWS)

TILES = dict(
    ffn_tm=1024, ffn_tf=512, ffn_tn=512, load_pc=32,
    pool_tt=512, pool_bb=16,
    conv_tm=1024, conv_tn=512,
    proj_tm=1024, proj_tn=1024,
)


def _params(sem):
    return pltpu.CompilerParams(dimension_semantics=sem, vmem_limit_bytes=VMEM_LIMIT)


def _rms(x, g):
    inv = lax.rsqrt(jnp.mean(x * x, axis=-1, keepdims=True) + RMS_EPS)
    return (x * inv) * g


ROW_CHUNK = 2 * BF16_SUBLANES


def _for_row_chunks(n_rows, fn):
    rc = min(ROW_CHUNK, n_rows)

    def body(r, carry):
        fn(pl.ds(pl.multiple_of(r * rc, rc), rc))
        return carry

    lax.fori_loop(0, n_rows // rc, body, 0)


LOAD_SLOTS = 3


def _load_rows_normed(src_rows, u_ref, g_ref, xbuf_ref, sem, n_chunks, pc):
    def copy(c):
        slot = c % LOAD_SLOTS
        return pltpu.make_async_copy(src_rows(c), xbuf_ref.at[slot], sem.at[slot])

    for c in range(min(LOAD_SLOTS - 1, n_chunks)):
        copy(c).start()

    def chunk_body(c, carry):
        @pl.when(c + LOAD_SLOTS - 1 < n_chunks)
        def _():
            copy(c + LOAD_SLOTS - 1).start()

        copy(c).wait()
        slot = c % LOAD_SLOTS
        for r in range(0, pc, ROW_CHUNK):
            rows = pl.ds(pl.multiple_of(c * pc + r, ROW_CHUNK), ROW_CHUNK)
            u_ref[rows, :] = _rms(xbuf_ref[slot, r:r + ROW_CHUNK, :], g_ref[...]).astype(BF16)
        return carry

    lax.fori_loop(0, n_chunks, chunk_body, 0)


def _ffn_kernel(*refs, n_cast, tm, tn, pc, final):
    x_hbm, xr_ref, xn_ref, g_ref, gf_ref, wup_ref, wdn_ref = refs[:7]
    cast_in = refs[7:7 + n_cast]
    o_hbm = refs[7 + n_cast]
    cast_out = refs[8 + n_cast:8 + 2 * n_cast]
    acc_ref, u_ref, xbuf_ref, in_sem, out_sem = refs[8 + 2 * n_cast:]
    i, j = pl.program_id(0), pl.program_id(1)
    ni, nj = pl.num_programs(0), pl.num_programs(1)
    rr = xr_ref.shape[0]

    def tile_writeback(tile):
        rows = pl.ds(pl.multiple_of(tile * tm, tm), tm)
        return pltpu.make_async_copy(acc_ref, o_hbm.at[rows, :], out_sem.at[0])

    @pl.when(j == 0)
    def _():
        @pl.when(i == 0)
        def _():
            def src_rows(c):
                return x_hbm.at[pl.ds(pl.multiple_of(c * pc, pc), pc), :]

            _load_rows_normed(src_rows, u_ref.at[0], g_ref, xbuf_ref, in_sem, tm // pc, pc)

        @pl.when(i > 0)
        def _():
            tile_writeback(i - 1).wait()

        def zero_rows(rows):
            acc_ref[rows, :] = jnp.zeros((rows.size, acc_ref.shape[1]), F32)

        _for_row_chunks(tm, zero_rows)

    h = jnp.dot(u_ref[i % 2], wup_ref[...], preferred_element_type=F32)
    h = jnp.maximum(h, 0.0)
    h = (h * h).astype(BF16)
    casts = list(zip(cast_in, cast_out))
    for k, n in enumerate(range(0, acc_ref.shape[1], tn)):
        acc_ref[:, n:n + tn] += jnp.dot(h, wdn_ref[:, n:n + tn], preferred_element_type=F32)
        if k < len(casts):
            casts[k][1][...] = casts[k][0][...].astype(BF16)
    for src, dst in casts[acc_ref.shape[1] // tn:]:
        dst[...] = src[...].astype(BF16)
    slab = pl.ds(pl.multiple_of(j * rr, rr), rr)
    acc_ref[slab, :] += xr_ref[...]
    u_ref[(i + 1) % 2, slab, :] = _rms(xn_ref[...], g_ref[...]).astype(BF16)

    @pl.when(j == nj - 1)
    def _():
        if final:
            def final_rows(rows):
                acc_ref[rows, :] = _rms(acc_ref[rows, :], gf_ref[...])

            _for_row_chunks(tm, final_rows)
        tile_writeback(i).start()

        @pl.when(i == ni - 1)
        def _():
            tile_writeback(i).wait()


def _cast_specs(w, layer, n_steps, nj):
    _, r, c = w.shape
    reps = 1
    while (r * reps) % n_steps or (r * reps // n_steps) % BF16_SUBLANES:
        reps *= 2
        assert reps <= n_steps, (w.shape, n_steps)
    rb = r * reps // n_steps
    in_spec = pl.BlockSpec((None, rb, c), lambda i, j: (layer, (i * nj + j) // reps, 0))
    out_spec = pl.BlockSpec((rb, c), lambda i, j: ((i * nj + j) // reps, 0))
    return in_spec, out_spec, jax.ShapeDtypeStruct((r, c), BF16)


def _ffn(x, g, gf, wup, wdn, cast_jobs=(), *, final):
    t, d = x.shape
    f = wup.shape[1]
    tm, tf, tn = min(TILES["ffn_tm"], t), min(TILES["ffn_tf"], f), min(TILES["ffn_tn"], d)
    pc = min(TILES["load_pc"], tm)
    ni, nj = t // tm, f // tf
    rr = tm // nj
    assert rr % BF16_SUBLANES == 0 and pc % ROW_CHUNK == 0, (tm, nj, pc)
    cast = [_cast_specs(w, layer, ni * nj, nj) for w, layer in cast_jobs]
    outs = pl.pallas_call(
        functools.partial(_ffn_kernel, n_cast=len(cast), tm=tm, tn=tn, pc=pc, final=final),
        grid=(ni, nj),
        in_specs=[
            pl.BlockSpec(memory_space=pl.ANY),
            pl.BlockSpec((rr, d), lambda i, j: (i * nj + j, 0)),
            pl.BlockSpec((rr, d), lambda i, j: (jnp.minimum(i + 1, ni - 1) * nj + j, 0)),
            pl.BlockSpec((1, d), lambda i, j: (0, 0)),
            pl.BlockSpec((1, d), lambda i, j: (0, 0)),
            pl.BlockSpec((d, tf), lambda i, j: (0, j)),
            pl.BlockSpec((tf, d), lambda i, j: (j, 0)),
        ] + [c[0] for c in cast],
        out_specs=[pl.BlockSpec(memory_space=pl.ANY)] + [c[1] for c in cast],
        out_shape=[jax.ShapeDtypeStruct((t, d), F32)] + [c[2] for c in cast],
        scratch_shapes=[
            pltpu.VMEM((tm, d), F32),
            pltpu.VMEM((2, tm, d), BF16),
            pltpu.VMEM((LOAD_SLOTS, pc, d), F32),
            pltpu.SemaphoreType.DMA((LOAD_SLOTS,)),
            pltpu.SemaphoreType.DMA((1,)),
        ],
        compiler_params=_params(("arbitrary", "arbitrary")),
        name="ffn",
    )(x, x, x, g, gf, wup, wdn, *[w for w, _ in cast_jobs])
    return outs[0], list(outs[1:])


def _pool_prompt_kernel(x_ref, xh_ref, g_ref, w_ref, sc_ref, o_ref, st_ref, p_ref, *, tt):
    t = pl.program_id(1)
    g = g_ref[...]
    d = x_ref.shape[2]
    gd = d // N_GROUPS
    for r in range(0, tt, ROW_CHUNK):
        p_ref[HALO + r:HALO + r + ROW_CHUNK, :] = _rms(x_ref[0, r:r + ROW_CHUNK, :], g)

    @pl.when(t == 0)
    def _():
        p_ref[0:HALO, :] = jnp.zeros((HALO, d), F32)

    @pl.when(t > 0)
    def _():
        p_ref[0:HALO, :] = _rms(xh_ref[0], g)

    pos = t * tt + lax.broadcasted_iota(jnp.int32, (tt, 1), 0)
    for gi, w in enumerate(POOL_WINDOWS):
        sl = slice(gi * gd, (gi + 1) * gd)
        u = p_ref[HALO:HALO + tt, sl]
        s = p_ref[0:HALO + tt, sl]
        k = 1
        while k < w:
            s = s + pltpu.roll(s, k, 0)
            k *= 2
        s = s[HALO:HALO + tt]
        cnt = jnp.minimum(pos + 1, w).astype(F32)
        diff = (s / cnt - u).astype(BF16)
        y = jnp.dot(diff, w_ref[gi], preferred_element_type=F32) * sc_ref[:, sl]
        o_ref[0, :, sl] = x_ref[0, :, sl] + y
    st_ref[0] = p_ref[HALO + tt - POOL_STATE_LEN:HALO + tt, :]


def _pool_prompt(x, g, w, layer, sc):
    b, s, d = x.shape
    gd = d // N_GROUPS
    tt = min(TILES["pool_tt"], s)
    hb = tt // HALO
    return pl.pallas_call(
        functools.partial(_pool_prompt_kernel, tt=tt),
        grid=(b, s // tt),
        in_specs=[
            pl.BlockSpec((1, tt, d), lambda i, t: (i, t, 0)),
            pl.BlockSpec((1, HALO, d), lambda i, t: (i, jnp.maximum(t * hb - 1, 0), 0)),
            pl.BlockSpec((1, d), lambda i, t: (0, 0)),
            pl.BlockSpec((None, N_GROUPS, gd, gd), lambda i, t: (layer, 0, 0, 0), pipeline_mode=pl.Buffered(1)),
            pl.BlockSpec((1, d), lambda i, t: (0, 0)),
        ],
        out_specs=[
            pl.BlockSpec((1, tt, d), lambda i, t: (i, t, 0)),
            pl.BlockSpec((1, POOL_STATE_LEN, d), lambda i, t: (i, 0, 0)),
        ],
        out_shape=[
            jax.ShapeDtypeStruct((b, s, d), F32),
            jax.ShapeDtypeStruct((b, POOL_STATE_LEN, d), F32),
        ],
        scratch_shapes=[pltpu.VMEM((HALO + tt, d), F32)],
        compiler_params=_params(("arbitrary", "arbitrary")),
        name="pool_prompt",
    )(x, x, g, w, sc)


def _pool_sample_kernel(x_ref, s_ref, g_ref, w_ref, sc_ref, *rest, bb, n_t, layer, first):
    o_ref, so_all_ref, u_ref, d_ref = rest[-4:]
    if first:
        so_ref = so_all_ref.at[layer]
        for other in range(so_all_ref.shape[0]):
            if other != layer:
                so_all_ref[other] = jnp.zeros(so_all_ref.shape[1:], F32)
    else:
        so_ref = so_all_ref
    d = u_ref.shape[2]
    gd = d // N_GROUPS
    g = g_ref[...]
    for t in range(n_t):
        u_ref[t] = _rms(x_ref[t], g)

    def padded_row(idx, sl):
        if idx < POOL_STATE_LEN:
            return s_ref[idx, :, sl]
        return u_ref[idx - POOL_STATE_LEN, :, sl]

    for gi, w in enumerate(POOL_WINDOWS):
        sl = slice(gi * gd, (gi + 1) * gd)
        for t in range(n_t):
            s = padded_row(POOL_STATE_LEN + t, sl)
            for k in range(1, w):
                s = s + padded_row(POOL_STATE_LEN + t - k, sl)
            cnt = float(min(PAST_LEN + t + 1, w))
            d_ref[t * bb:(t + 1) * bb, sl] = (s / cnt - u_ref[t, :, sl]).astype(BF16)
        y = jnp.dot(d_ref[:, sl], w_ref[gi], preferred_element_type=F32) * sc_ref[:, sl]
        for t in range(n_t):
            o_ref[t, :, sl] = x_ref[t, :, sl] + y[t * bb:(t + 1) * bb]

    keep = POOL_STATE_LEN - n_t
    for k in range(keep):
        so_ref[k] = s_ref[n_t + k]
    for t in range(n_t):
        so_ref[keep + t] = u_ref[t]


def _pool_sample(x, state, g, w, layer, sc, states_out):
    n_t, b, d = x.shape
    gd = d // N_GROUPS
    assert n_t <= POOL_STATE_LEN
    bb = min(TILES["pool_bb"], b)
    prev = [] if states_out is None else [states_out]
    n_layers = w.shape[0]
    if prev:
        so_spec = pl.BlockSpec((None, POOL_STATE_LEN, bb, d), lambda i: (layer, 0, i, 0))
    else:
        so_spec = pl.BlockSpec((n_layers, POOL_STATE_LEN, bb, d), lambda i: (0, 0, i, 0))
    return pl.pallas_call(
        functools.partial(_pool_sample_kernel, bb=bb, n_t=n_t, layer=layer, first=not prev),
        grid=(b // bb,),
        in_specs=[
            pl.BlockSpec((n_t, bb, d), lambda i: (0, i, 0)),
            pl.BlockSpec((None, POOL_STATE_LEN, bb, d), lambda i: (layer, 0, i, 0)),
            pl.BlockSpec((1, d), lambda i: (0, 0)),
            pl.BlockSpec((None, N_GROUPS, gd, gd), lambda i: (layer, 0, 0, 0), pipeline_mode=pl.Buffered(1)),
            pl.BlockSpec((1, d), lambda i: (0, 0)),
        ] + [pl.BlockSpec(memory_space=pl.ANY) for _ in prev],
        out_specs=[
            pl.BlockSpec((n_t, bb, d), lambda i: (0, i, 0)),
            so_spec,
        ],
        out_shape=[
            jax.ShapeDtypeStruct((n_t, b, d), F32),
            jax.ShapeDtypeStruct((n_layers, POOL_STATE_LEN, b, d), F32),
        ],
        input_output_aliases={5: 1} if prev else {},
        scratch_shapes=[pltpu.VMEM((n_t, bb, d), F32), pltpu.VMEM((n_t * bb, d), BF16)],
        compiler_params=_params(("arbitrary",)),
        name="pool_sample",
    )(x, state, g, w, sc, *prev)


def _conv_in_prompt_kernel(x_hbm, xn_ref, g_ref, wb_ref, wc_ref, wh_ref, cw_ref, o_ref, st_ref,
                           u_ref, v_ref, carry_ref, xbuf_ref, in_sem, *, tm, pc):
    t = pl.program_id(1)
    c = pl.program_id(2)
    tile = pl.program_id(0) * pl.num_programs(1) + t
    sub = F32_SUBLANES

    @pl.when((tile == 0) & (c == 0))
    def _():
        def src_rows(k):
            return x_hbm.at[0, pl.ds(pl.multiple_of(k * pc, pc), pc), :]

        _load_rows_normed(src_rows, u_ref.at[0], g_ref, xbuf_ref, in_sem, tm // pc, pc)

    u = u_ref[tile % 2]
    zc = jnp.dot(u, wc_ref[...], preferred_element_type=F32)
    zh = jnp.dot(u, wh_ref[...], preferred_element_type=F32)
    v_ref[sub:sub + tm, :] = zc * zh
    v_ref[0:sub, :] = jnp.where(t > 0, carry_ref[c], 0.0)

    conv = cw_ref[CONV_WIDTH - 1:CONV_WIDTH, :] * v_ref[sub:sub + tm, :]
    for k in range(CONV_WIDTH - 1):
        off = sub - (CONV_WIDTH - 1) + k
        conv = conv + cw_ref[k:k + 1, :] * v_ref[off:off + tm, :]
    zb = jnp.dot(u, wb_ref[...], preferred_element_type=F32)
    o_ref[0] = (zb * conv).astype(BF16)
    carry_ref[c] = v_ref[tm:tm + sub, :]

    rn = xn_ref.shape[1]
    rc = min(ROW_CHUNK, rn)
    for r in range(0, rn, rc):
        rows = pl.ds(pl.multiple_of(c * rn + r, rc), rc)
        u_ref[(tile + 1) % 2, rows, :] = _rms(xn_ref[0, r:r + rc, :], g_ref[...]).astype(BF16)

    @pl.when(t == pl.num_programs(1) - 1)
    def _():
        st_ref[0, c] = v_ref[sub + tm - CONV_STATE_LEN:sub + tm, :]


def _conv_in_prompt(x, g, w_in, layer, cw):
    b, s, d = x.shape
    tm, tn = min(TILES["conv_tm"], s), min(TILES["conv_tn"], d)
    pc = min(TILES["load_pc"], tm)
    nc = d // tn
    nt = s // tm
    rn = tm // nc
    assert rn % BF16_SUBLANES == 0, (tm, nc)

    def next_slab(i, t, c):
        nxt = jnp.minimum(i * nt + t + 1, b * nt - 1)
        return (nxt // nt, (nxt % nt) * nc + c, 0)

    gated, st = pl.pallas_call(
        functools.partial(_conv_in_prompt_kernel, tm=tm, pc=pc),
        grid=(b, nt, nc),
        in_specs=[
            pl.BlockSpec(memory_space=pl.ANY),
            pl.BlockSpec((1, rn, d), next_slab),
            pl.BlockSpec((1, d), lambda i, t, c: (0, 0)),
            pl.BlockSpec((d, tn), lambda i, t, c: (0, c)),
            pl.BlockSpec((d, tn), lambda i, t, c: (0, nc + c)),
            pl.BlockSpec((d, tn), lambda i, t, c: (0, 2 * nc + c)),
            pl.BlockSpec((None, CONV_WIDTH, tn), lambda i, t, c: (layer, 0, c)),
        ],
        out_specs=[
            pl.BlockSpec((1, tm, tn), lambda i, t, c: (i, t, c)),
            pl.BlockSpec((1, nc, CONV_STATE_LEN, tn), lambda i, t, c: (i, 0, 0, 0)),
        ],
        out_shape=[
            jax.ShapeDtypeStruct((b, s, d), BF16),
            jax.ShapeDtypeStruct((b, nc, CONV_STATE_LEN, tn), F32),
        ],
        scratch_shapes=[
            pltpu.VMEM((2, tm, d), BF16),
            pltpu.VMEM((F32_SUBLANES + tm, tn), F32),
            pltpu.VMEM((nc, F32_SUBLANES, tn), F32),
            pltpu.VMEM((LOAD_SLOTS, pc, d), F32),
            pltpu.SemaphoreType.DMA((LOAD_SLOTS,)),
        ],
        compiler_params=_params(("arbitrary", "arbitrary", "arbitrary")),
        name="conv_in_prompt",
    )(x, x, g, w_in, w_in, w_in, cw)
    return gated, st.transpose(0, 2, 1, 3).reshape(b, CONV_STATE_LEN, d)


def _conv_in_sample_kernel(x_ref, s0_ref, s1_ref, g_ref, wb_ref, wc_ref, wh_ref, cw_ref,
                           o_ref, st0_ref, st1_ref, u_ref, *, b, n_t):
    c = pl.program_id(0)

    @pl.when(c == 0)
    def _():
        for t in range(n_t):
            u_ref[t * b:(t + 1) * b, :] = _rms(x_ref[t], g_ref[...]).astype(BF16)

    u = u_ref[...]
    zc = jnp.dot(u, wc_ref[...], preferred_element_type=F32)
    zh = jnp.dot(u, wh_ref[...], preferred_element_type=F32)
    zb = jnp.dot(u, wb_ref[...], preferred_element_type=F32)
    v = zc * zh
    rows = [s0_ref[...], s1_ref[...]] + [v[t * b:(t + 1) * b] for t in range(n_t)]
    for t in range(n_t):
        conv = cw_ref[0:1, :] * rows[t]
        for k in range(1, CONV_WIDTH):
            conv = conv + cw_ref[k:k + 1, :] * rows[t + k]
        o_ref[t * b:(t + 1) * b, :] = (zb[t * b:(t + 1) * b] * conv).astype(BF16)
    st0_ref[...] = rows[n_t]
    st1_ref[...] = rows[n_t + 1]


def _conv_in_sample(x, state, g, w_in, layer, cw):
    n_t, b, d = x.shape
    tn = min(TILES["conv_tn"], d)
    nc = d // tn
    assert CONV_STATE_LEN == 2
    return pl.pallas_call(
        functools.partial(_conv_in_sample_kernel, b=b, n_t=n_t),
        grid=(nc,),
        in_specs=[
            pl.BlockSpec((n_t, b, d), lambda c: (0, 0, 0)),
            pl.BlockSpec((b, tn), lambda c: (0, c)),
            pl.BlockSpec((b, tn), lambda c: (0, nc + c)),
            pl.BlockSpec((1, d), lambda c: (0, 0)),
            pl.BlockSpec((d, tn), lambda c: (0, c)),
            pl.BlockSpec((d, tn), lambda c: (0, nc + c)),
            pl.BlockSpec((d, tn), lambda c: (0, 2 * nc + c)),
            pl.BlockSpec((None, CONV_WIDTH, tn), lambda c: (layer, 0, c)),
        ],
        out_specs=[
            pl.BlockSpec((n_t * b, tn), lambda c: (0, c)),
            pl.BlockSpec((b, tn), lambda c: (0, c)),
            pl.BlockSpec((b, tn), lambda c: (0, c)),
        ],
        out_shape=[
            jax.ShapeDtypeStruct((n_t * b, d), BF16),
            jax.ShapeDtypeStruct((b, d), F32),
            jax.ShapeDtypeStruct((b, d), F32),
        ],
        scratch_shapes=[pltpu.VMEM((n_t * b, d), BF16)],
        compiler_params=_params(("arbitrary",)),
        name="conv_in_sample",
    )(x, state, state, g, w_in, w_in, w_in, cw)


def _proj_res_kernel(a_ref, w_ref, x_ref, o_ref):
    o_ref[...] = x_ref[...] + jnp.dot(a_ref[...], w_ref[...], preferred_element_type=F32)


def _conv_out(a, w, x):
    t, d = x.shape
    tm, tn = min(TILES["proj_tm"], t), min(TILES["proj_tn"], d)
    return pl.pallas_call(
        _proj_res_kernel,
        grid=(t // tm, d // tn),
        in_specs=[
            pl.BlockSpec((tm, d), lambda i, n: (i, 0)),
            pl.BlockSpec((d, tn), lambda i, n: (0, n)),
            pl.BlockSpec((tm, tn), lambda i, n: (i, n)),
        ],
        out_specs=pl.BlockSpec((tm, tn), lambda i, n: (i, n)),
        out_shape=jax.ShapeDtypeStruct((t, d), F32),
        compiler_params=_params(("arbitrary", "arbitrary")),
        name="conv_out",
    )(a, w, x)


def kernel(x_prompt, x_sample, state_pool, state_conv, norm_mix, norm_mlp, norm_final,
           w_pool, pool_scale, w_conv_in, conv_w, w_conv_out, w_up, w_down):
    bp, sp, d = x_prompt.shape
    bs, ss, _ = x_sample.shape

    w_pool_b = w_pool.astype(BF16)
    wup_b, wdn_b = w_up[0].astype(BF16), w_down[0].astype(BF16)
    wci_b = wco_b = None
    g_fin = norm_final.reshape(1, d)

    xp = x_prompt
    xs = x_sample.transpose(1, 0, 2)
    pool_p, conv_p, conv_s = [], [], []
    pool_s = None
    state_pool_t = state_pool.transpose(0, 2, 1, 3)
    for i in range(DEPTH):
        j = i // 2
        last = i == DEPTH - 1
        g_mix = norm_mix[i].reshape(1, d)
        if i % 2 == 0:
            sc = pool_scale[j].reshape(1, d)
            xp, st = _pool_prompt(xp, g_mix, w_pool_b, j, sc)
            pool_p.append(st)
            xs, pool_s = _pool_sample(xs, state_pool_t, g_mix, w_pool_b, j, sc, pool_s)
        else:
            a, st = _conv_in_prompt(xp, g_mix, wci_b, j, conv_w)
            conv_p.append(st)
            xp = _conv_out(a.reshape(bp * sp, d), wco_b, xp.reshape(bp * sp, d)).reshape(bp, sp, d)
            a, st0, st1 = _conv_in_sample(xs, state_conv[j].reshape(bs, CONV_STATE_LEN * d), g_mix,
                                          wci_b, j, conv_w)
            conv_s.append(jnp.stack([st0, st1], axis=1))
            xs = _conv_out(a, wco_b, xs.reshape(ss * bs, d)).reshape(ss, bs, d)
        g_mlp = norm_mlp[i].reshape(1, d)
        jobs = []
        if not last:
            if (i + 1) % 2 == 1:
                jobs += [(w_conv_in, (i + 1) // 2), (w_conv_out, (i + 1) // 2)]
            jobs += [(w_up, i + 1), (w_down, i + 1)]
        xp, cast = _ffn(xp.reshape(bp * sp, d), g_mlp, g_fin, wup_b, wdn_b, jobs, final=last)
        xp = xp.reshape(bp, sp, d)
        xs, _ = _ffn(xs.reshape(ss * bs, d), g_mlp, g_fin, wup_b, wdn_b, final=last)
        xs = xs.reshape(ss, bs, d)
        if not last:
            if (i + 1) % 2 == 1:
                wci_b, wco_b = cast[0], cast[1]
            wup_b, wdn_b = cast[-2], cast[-1]

    y_sample = xs.transpose(1, 0, 2)
    return (xp, y_sample, jnp.stack(pool_p), pool_s.transpose(0, 2, 1, 3), jnp.stack(conv_p), jnp.stack(conv_s))
```

```python
import functools

import jax
import jax.numpy as jnp
from jax import lax
from jax.experimental import pallas as pl
from jax.experimental.pallas import tpu as pltpu

DEPTH = 4
PAST_LEN = 16384
POOL_WINDOWS = (2, 4, 8, 16)
N_GROUPS = len(POOL_WINDOWS)
POOL_STATE_LEN = max(POOL_WINDOWS) - 1
CONV_WIDTH = 3
CONV_STATE_LEN = CONV_WIDTH - 1
RMS_EPS = 1e-6

F32 = jnp.float32
BF16 = jnp.bfloat16

V7X_VMEM_BYTES = 64 * 1024 * 1024
VMEM_LIMIT = V7X_VMEM_BYTES - 1024 * 1024
F32_SUBLANES = 8
BF16_SUBLANES = 16
HALO = 2 * F32_SUBLANES
assert POOL_STATE_LEN <= HALO and all(w & (w - 1) == 0 for w in POOL_WINDOWS)

TILES = dict(
    ffn_tm=1024, ffn_tf=512, ffn_tf_f32=256, ffn_tn=512, load_pc=32,
    pool_tt=512, pool_bb=16,
    conv_tm=1024, conv_tn=512,
    proj_tm=1024, proj_tn=1024,
)


def _params(sem):
    return pltpu.CompilerParams(dimension_semantics=sem, vmem_limit_bytes=VMEM_LIMIT)


def _rms(x, g):
    inv = lax.rsqrt(jnp.mean(x * x, axis=-1, keepdims=True) + RMS_EPS)
    return (x * inv) * g


ROW_CHUNK = 2 * BF16_SUBLANES


def _for_row_chunks(n_rows, fn):
    rc = min(ROW_CHUNK, n_rows)

    def body(r, carry):
        fn(pl.ds(pl.multiple_of(r * rc, rc), rc))
        return carry

    lax.fori_loop(0, n_rows // rc, body, 0)


LOAD_SLOTS = 3


def _load_rows_normed(src_rows, u_ref, g_ref, xbuf_ref, sem, n_chunks, pc):
    def copy(c):
        slot = c % LOAD_SLOTS
        return pltpu.make_async_copy(src_rows(c), xbuf_ref.at[slot], sem.at[slot])

    for c in range(min(LOAD_SLOTS - 1, n_chunks)):
        copy(c).start()

    def chunk_body(c, carry):
        @pl.when(c + LOAD_SLOTS - 1 < n_chunks)
        def _():
            copy(c + LOAD_SLOTS - 1).start()

        copy(c).wait()
        slot = c % LOAD_SLOTS
        for r in range(0, pc, ROW_CHUNK):
            rows = pl.ds(pl.multiple_of(c * pc + r, ROW_CHUNK), ROW_CHUNK)
            u_ref[rows, :] = _rms(xbuf_ref[slot, r:r + ROW_CHUNK, :], g_ref[...]).astype(BF16)
        return carry

    lax.fori_loop(0, n_chunks, chunk_body, 0)


def _ffn_kernel(*refs, n_cast, tm, tn, pc, final, emit_weights, multi_tile):
    x_hbm, xr_ref, xn_ref, g_ref, gf_ref, wup_ref, wdn_ref = refs[:7]
    cast_in = refs[7:7 + n_cast]
    o_hbm = refs[7 + n_cast]
    cast_out = refs[8 + n_cast:8 + 2 * n_cast]
    n_w = 2 if emit_weights else 0
    w_out = refs[8 + 2 * n_cast:8 + 2 * n_cast + n_w]
    acc_ref, u_ref, xbuf_ref, in_sem, out_sem = refs[8 + 2 * n_cast + n_w:]
    i, j = pl.program_id(0), pl.program_id(1)
    ni, nj = pl.num_programs(0), pl.num_programs(1)
    rr = xr_ref.shape[0]

    def tile_writeback(tile):
        rows = pl.ds(pl.multiple_of(tile * tm, tm), tm)
        return pltpu.make_async_copy(acc_ref, o_hbm.at[rows, :], out_sem.at[0])

    @pl.when(j == 0)
    def _():
        @pl.when(i == 0)
        def _():
            def src_rows(c):
                return x_hbm.at[pl.ds(pl.multiple_of(c * pc, pc), pc), :]

            _load_rows_normed(src_rows, u_ref.at[0], g_ref, xbuf_ref, in_sem, tm // pc, pc)

        @pl.when(i > 0)
        def _():
            tile_writeback(i - 1).wait()

        def zero_rows(rows):
            acc_ref[rows, :] = jnp.zeros((rows.size, acc_ref.shape[1]), F32)

        _for_row_chunks(tm, zero_rows)

    def weight(ref, out_ref, idx):
        w = ref[idx]
        if emit_weights:
            w = w.astype(BF16)
            out_ref[idx] = w
        return w

    h = jnp.dot(u_ref[i % 2], weight(wup_ref, w_out[0] if emit_weights else None, (slice(None), slice(None))),
                preferred_element_type=F32)
    h = jnp.maximum(h, 0.0)
    h = (h * h).astype(BF16)
    casts = list(zip(cast_in, cast_out))
    for k, n in enumerate(range(0, acc_ref.shape[1], tn)):
        wdn = weight(wdn_ref, w_out[1] if emit_weights else None, (slice(None), slice(n, n + tn)))
        acc_ref[:, n:n + tn] += jnp.dot(h, wdn, preferred_element_type=F32)
        if k < len(casts):
            casts[k][1][...] = casts[k][0][...].astype(BF16)
    for src, dst in casts[acc_ref.shape[1] // tn:]:
        dst[...] = src[...].astype(BF16)
    slab = pl.ds(pl.multiple_of(j * rr, rr), rr)
    acc_ref[slab, :] += xr_ref[...]
    if multi_tile:
        u_ref[(i + 1) % 2, slab, :] = _rms(xn_ref[...], g_ref[...]).astype(BF16)

    @pl.when(j == nj - 1)
    def _():
        if final:
            def final_rows(rows):
                acc_ref[rows, :] = _rms(acc_ref[rows, :], gf_ref[...])

            _for_row_chunks(tm, final_rows)
        tile_writeback(i).start()

        @pl.when(i == ni - 1)
        def _():
            tile_writeback(i).wait()


def _cast_specs(w, layer, n_steps, nj):
    _, r, c = w.shape
    reps = 1
    while (r * reps) % n_steps or (r * reps // n_steps) % BF16_SUBLANES:
        reps *= 2
        assert reps <= n_steps, (w.shape, n_steps)
    rb = r * reps // n_steps
    in_spec = pl.BlockSpec((None, rb, c), lambda i, j: (layer, (i * nj + j) // reps, 0))
    out_spec = pl.BlockSpec((rb, c), lambda i, j: ((i * nj + j) // reps, 0))
    return in_spec, out_spec, jax.ShapeDtypeStruct((r, c), BF16)


def _ffn(x, g, gf, wup, wdn, cast_jobs=(), *, final):
    t, d = x.shape
    emit_weights = isinstance(wup, tuple)
    if emit_weights:
        (wup, lu), (wdn, ld) = wup, wdn
        f = wup.shape[2]
        tf = min(TILES["ffn_tf_f32"], f)
        wup_spec = pl.BlockSpec((None, d, tf), lambda i, j: (lu, 0, j))
        wdn_spec = pl.BlockSpec((None, tf, d), lambda i, j: (ld, j, 0))
    else:
        f = wup.shape[1]
        tf = min(TILES["ffn_tf"], f)
        wup_spec = pl.BlockSpec((d, tf), lambda i, j: (0, j))
        wdn_spec = pl.BlockSpec((tf, d), lambda i, j: (j, 0))
    tm, tn = min(TILES["ffn_tm"], t), min(TILES["ffn_tn"], d)
    pc = min(TILES["load_pc"], tm)
    ni, nj = t // tm, f // tf
    rr = tm // nj
    assert rr % (BF16_SUBLANES if ni > 1 else F32_SUBLANES) == 0 and pc % ROW_CHUNK == 0, (tm, nj, pc)
    assert ni == 1 or not emit_weights
    cast = [_cast_specs(w, layer, ni * nj, nj) for w, layer in cast_jobs]
    w_specs, w_shapes = [], []
    if emit_weights:
        w_specs = [pl.BlockSpec((d, tf), lambda i, j: (0, j)), pl.BlockSpec((tf, d), lambda i, j: (j, 0))]
        w_shapes = [jax.ShapeDtypeStruct((d, f), BF16), jax.ShapeDtypeStruct((f, d), BF16)]
    outs = pl.pallas_call(
        functools.partial(_ffn_kernel, n_cast=len(cast), tm=tm, tn=tn, pc=pc, final=final,
                          emit_weights=emit_weights, multi_tile=ni > 1),
        grid=(ni, nj),
        in_specs=[
            pl.BlockSpec(memory_space=pl.ANY),
            pl.BlockSpec((rr, d), lambda i, j: (i * nj + j, 0)),
            pl.BlockSpec((rr, d), lambda i, j: (jnp.minimum(i + 1, ni - 1) * nj + j, 0)),
            pl.BlockSpec((1, d), lambda i, j: (0, 0)),
            pl.BlockSpec((1, d), lambda i, j: (0, 0)),
            wup_spec,
            wdn_spec,
        ] + [c[0] for c in cast],
        out_specs=[pl.BlockSpec(memory_space=pl.ANY)] + [c[1] for c in cast] + w_specs,
        out_shape=[jax.ShapeDtypeStruct((t, d), F32)] + [c[2] for c in cast] + w_shapes,
        scratch_shapes=[
            pltpu.VMEM((tm, d), F32),
            pltpu.VMEM((2, tm, d), BF16),
            pltpu.VMEM((LOAD_SLOTS, pc, d), F32),
            pltpu.SemaphoreType.DMA((LOAD_SLOTS,)),
            pltpu.SemaphoreType.DMA((1,)),
        ],
        compiler_params=_params(("arbitrary", "arbitrary")),
        name="ffn",
    )(x, x, x, g, gf, wup, wdn, *[w for w, _ in cast_jobs])
    return outs[0], list(outs[1:])


def _pool_prompt_kernel(x_ref, xh_ref, g_ref, w_ref, sc_ref, o_ref, st_ref, p_ref, *, tt):
    t = pl.program_id(1)
    g = g_ref[...]
    d = x_ref.shape[2]
    gd = d // N_GROUPS
    for r in range(0, tt, ROW_CHUNK):
        p_ref[HALO + r:HALO + r + ROW_CHUNK, :] = _rms(x_ref[0, r:r + ROW_CHUNK, :], g)

    @pl.when(t == 0)
    def _():
        p_ref[0:HALO, :] = jnp.zeros((HALO, d), F32)

    @pl.when(t > 0)
    def _():
        p_ref[0:HALO, :] = _rms(xh_ref[0], g)

    pos = t * tt + lax.broadcasted_iota(jnp.int32, (tt, 1), 0)
    for gi, w in enumerate(POOL_WINDOWS):
        sl = slice(gi * gd, (gi + 1) * gd)
        u = p_ref[HALO:HALO + tt, sl]
        s = p_ref[0:HALO + tt, sl]
        k = 1
        while k < w:
            s = s + pltpu.roll(s, k, 0)
            k *= 2
        s = s[HALO:HALO + tt]
        cnt = jnp.minimum(pos + 1, w).astype(F32)
        diff = (s / cnt - u).astype(BF16)
        y = jnp.dot(diff, w_ref[gi], preferred_element_type=F32) * sc_ref[:, sl]
        o_ref[0, :, sl] = x_ref[0, :, sl] + y
    st_ref[0] = p_ref[HALO + tt - POOL_STATE_LEN:HALO + tt, :]


def _pool_prompt(x, g, w, layer, sc):
    b, s, d = x.shape
    gd = d // N_GROUPS
    tt = min(TILES["pool_tt"], s)
    hb = tt // HALO
    return pl.pallas_call(
        functools.partial(_pool_prompt_kernel, tt=tt),
        grid=(b, s // tt),
        in_specs=[
            pl.BlockSpec((1, tt, d), lambda i, t: (i, t, 0)),
            pl.BlockSpec((1, HALO, d), lambda i, t: (i, jnp.maximum(t * hb - 1, 0), 0)),
            pl.BlockSpec((1, d), lambda i, t: (0, 0)),
            pl.BlockSpec((None, N_GROUPS, gd, gd), lambda i, t: (layer, 0, 0, 0), pipeline_mode=pl.Buffered(1)),
            pl.BlockSpec((1, d), lambda i, t: (0, 0)),
        ],
        out_specs=[
            pl.BlockSpec((1, tt, d), lambda i, t: (i, t, 0)),
            pl.BlockSpec((1, POOL_STATE_LEN, d), lambda i, t: (i, 0, 0)),
        ],
        out_shape=[
            jax.ShapeDtypeStruct((b, s, d), F32),
            jax.ShapeDtypeStruct((b, POOL_STATE_LEN, d), F32),
        ],
        scratch_shapes=[pltpu.VMEM((HALO + tt, d), F32)],
        compiler_params=_params(("arbitrary", "arbitrary")),
        name="pool_prompt",
    )(x, x, g, w, sc)


def _pool_sample_kernel(x_ref, s_ref, g_ref, w_ref, sc_ref, *rest, bb, n_t, layer, first):
    o_ref, so_all_ref, u_ref, d_ref = rest[-4:]
    if first:
        so_ref = so_all_ref.at[layer]
        for other in range(so_all_ref.shape[0]):
            if other != layer:
                so_all_ref[other] = jnp.zeros(so_all_ref.shape[1:], F32)
    else:
        so_ref = so_all_ref
    d = u_ref.shape[2]
    gd = d // N_GROUPS
    g = g_ref[...]
    for t in range(n_t):
        u_ref[t] = _rms(x_ref[t], g)

    def padded_row(idx, sl):
        if idx < POOL_STATE_LEN:
            return s_ref[idx, :, sl]
        return u_ref[idx - POOL_STATE_LEN, :, sl]

    for gi, w in enumerate(POOL_WINDOWS):
        sl = slice(gi * gd, (gi + 1) * gd)
        for t in range(n_t):
            s = padded_row(POOL_STATE_LEN + t, sl)
            for k in range(1, w):
                s = s + padded_row(POOL_STATE_LEN + t - k, sl)
            cnt = float(min(PAST_LEN + t + 1, w))
            d_ref[t * bb:(t + 1) * bb, sl] = (s / cnt - u_ref[t, :, sl]).astype(BF16)
        y = jnp.dot(d_ref[:, sl], w_ref[gi], preferred_element_type=F32) * sc_ref[:, sl]
        for t in range(n_t):
            o_ref[t, :, sl] = x_ref[t, :, sl] + y[t * bb:(t + 1) * bb]

    keep = POOL_STATE_LEN - n_t
    for k in range(keep):
        so_ref[k] = s_ref[n_t + k]
    for t in range(n_t):
        so_ref[keep + t] = u_ref[t]


def _pool_sample(x, state, g, w, layer, sc, states_out):
    n_t, b, d = x.shape
    gd = d // N_GROUPS
    assert n_t <= POOL_STATE_LEN
    bb = min(TILES["pool_bb"], b)
    prev = [] if states_out is None else [states_out]
    n_layers = w.shape[0]
    if prev:
        so_spec = pl.BlockSpec((None, POOL_STATE_LEN, bb, d), lambda i: (layer, 0, i, 0))
    else:
        so_spec = pl.BlockSpec((n_layers, POOL_STATE_LEN, bb, d), lambda i: (0, 0, i, 0))
    return pl.pallas_call(
        functools.partial(_pool_sample_kernel, bb=bb, n_t=n_t, layer=layer, first=not prev),
        grid=(b // bb,),
        in_specs=[
            pl.BlockSpec((n_t, bb, d), lambda i: (0, i, 0)),
            pl.BlockSpec((None, POOL_STATE_LEN, bb, d), lambda i: (layer, 0, i, 0)),
            pl.BlockSpec((1, d), lambda i: (0, 0)),
            pl.BlockSpec((None, N_GROUPS, gd, gd), lambda i: (layer, 0, 0, 0), pipeline_mode=pl.Buffered(1)),
            pl.BlockSpec((1, d), lambda i: (0, 0)),
        ] + [pl.BlockSpec(memory_space=pl.ANY) for _ in prev],
        out_specs=[
            pl.BlockSpec((n_t, bb, d), lambda i: (0, i, 0)),
            so_spec,
        ],
        out_shape=[
            jax.ShapeDtypeStruct((n_t, b, d), F32),
            jax.ShapeDtypeStruct((n_layers, POOL_STATE_LEN, b, d), F32),
        ],
        input_output_aliases={5: 1} if prev else {},
        scratch_shapes=[pltpu.VMEM((n_t, bb, d), F32), pltpu.VMEM((n_t * bb, d), BF16)],
        compiler_params=_params(("arbitrary",)),
        name="pool_sample",
    )(x, state, g, w, sc, *prev)


def _conv_in_prompt_kernel(x_hbm, xn_ref, g_ref, wb_ref, wc_ref, wh_ref, cw_ref, o_ref, st_ref,
                           u_ref, v_ref, carry_ref, xbuf_ref, in_sem, *, tm, pc):
    t = pl.program_id(1)
    c = pl.program_id(2)
    tile = pl.program_id(0) * pl.num_programs(1) + t
    sub = F32_SUBLANES

    @pl.when((tile == 0) & (c == 0))
    def _():
        def src_rows(k):
            return x_hbm.at[0, pl.ds(pl.multiple_of(k * pc, pc), pc), :]

        _load_rows_normed(src_rows, u_ref.at[0], g_ref, xbuf_ref, in_sem, tm // pc, pc)

    u = u_ref[tile % 2]
    zc = jnp.dot(u, wc_ref[...], preferred_element_type=F32)
    zh = jnp.dot(u, wh_ref[...], preferred_element_type=F32)
    v_ref[sub:sub + tm, :] = zc * zh
    v_ref[0:sub, :] = jnp.where(t > 0, carry_ref[c], 0.0)

    conv = cw_ref[CONV_WIDTH - 1:CONV_WIDTH, :] * v_ref[sub:sub + tm, :]
    for k in range(CONV_WIDTH - 1):
        off = sub - (CONV_WIDTH - 1) + k
        conv = conv + cw_ref[k:k + 1, :] * v_ref[off:off + tm, :]
    zb = jnp.dot(u, wb_ref[...], preferred_element_type=F32)
    o_ref[0] = (zb * conv).astype(BF16)
    carry_ref[c] = v_ref[tm:tm + sub, :]

    rn = xn_ref.shape[1]
    rc = min(ROW_CHUNK, rn)
    for r in range(0, rn, rc):
        rows = pl.ds(pl.multiple_of(c * rn + r, rc), rc)
        u_ref[(tile + 1) % 2, rows, :] = _rms(xn_ref[0, r:r + rc, :], g_ref[...]).astype(BF16)

    @pl.when(t == pl.num_programs(1) - 1)
    def _():
        st_ref[0, c] = v_ref[sub + tm - CONV_STATE_LEN:sub + tm, :]


def _conv_in_prompt(x, g, w_in, layer, cw):
    b, s, d = x.shape
    tm, tn = min(TILES["conv_tm"], s), min(TILES["conv_tn"], d)
    pc = min(TILES["load_pc"], tm)
    nc = d // tn
    nt = s // tm
    rn = tm // nc
    assert rn % BF16_SUBLANES == 0, (tm, nc)

    def next_slab(i, t, c):
        nxt = jnp.minimum(i * nt + t + 1, b * nt - 1)
        return (nxt // nt, (nxt % nt) * nc + c, 0)

    gated, st = pl.pallas_call(
        functools.partial(_conv_in_prompt_kernel, tm=tm, pc=pc),
        grid=(b, nt, nc),
        in_specs=[
            pl.BlockSpec(memory_space=pl.ANY),
            pl.BlockSpec((1, rn, d), next_slab),
            pl.BlockSpec((1, d), lambda i, t, c: (0, 0)),
            pl.BlockSpec((d, tn), lambda i, t, c: (0, c)),
            pl.BlockSpec((d, tn), lambda i, t, c: (0, nc + c)),
            pl.BlockSpec((d, tn), lambda i, t, c: (0, 2 * nc + c)),
            pl.BlockSpec((None, CONV_WIDTH, tn), lambda i, t, c: (layer, 0, c)),
        ],
        out_specs=[
            pl.BlockSpec((1, tm, tn), lambda i, t, c: (i, t, c)),
            pl.BlockSpec((1, nc, CONV_STATE_LEN, tn), lambda i, t, c: (i, 0, 0, 0)),
        ],
        out_shape=[
            jax.ShapeDtypeStruct((b, s, d), BF16),
            jax.ShapeDtypeStruct((b, nc, CONV_STATE_LEN, tn), F32),
        ],
        scratch_shapes=[
            pltpu.VMEM((2, tm, d), BF16),
            pltpu.VMEM((F32_SUBLANES + tm, tn), F32),
            pltpu.VMEM((nc, F32_SUBLANES, tn), F32),
            pltpu.VMEM((LOAD_SLOTS, pc, d), F32),
            pltpu.SemaphoreType.DMA((LOAD_SLOTS,)),
        ],
        compiler_params=_params(("arbitrary", "arbitrary", "arbitrary")),
        name="conv_in_prompt",
    )(x, x, g, w_in, w_in, w_in, cw)
    return gated, st.transpose(0, 2, 1, 3).reshape(b, CONV_STATE_LEN, d)


def _conv_in_sample_kernel(x_ref, s0_ref, s1_ref, g_ref, wb_ref, wc_ref, wh_ref, cw_ref,
                           o_ref, st0_ref, st1_ref, u_ref, *, b, n_t):
    c = pl.program_id(0)

    @pl.when(c == 0)
    def _():
        for t in range(n_t):
            u_ref[t * b:(t + 1) * b, :] = _rms(x_ref[t], g_ref[...]).astype(BF16)

    u = u_ref[...]
    zc = jnp.dot(u, wc_ref[...], preferred_element_type=F32)
    zh = jnp.dot(u, wh_ref[...], preferred_element_type=F32)
    zb = jnp.dot(u, wb_ref[...], preferred_element_type=F32)
    v = zc * zh
    rows = [s0_ref[...], s1_ref[...]] + [v[t * b:(t + 1) * b] for t in range(n_t)]
    for t in range(n_t):
        conv = cw_ref[0:1, :] * rows[t]
        for k in range(1, CONV_WIDTH):
            conv = conv + cw_ref[k:k + 1, :] * rows[t + k]
        o_ref[t * b:(t + 1) * b, :] = (zb[t * b:(t + 1) * b] * conv).astype(BF16)
    st0_ref[...] = rows[n_t]
    st1_ref[...] = rows[n_t + 1]


def _conv_in_sample(x, state, g, w_in, layer, cw):
    n_t, b, d = x.shape
    tn = min(TILES["conv_tn"], d)
    nc = d // tn
    assert CONV_STATE_LEN == 2
    return pl.pallas_call(
        functools.partial(_conv_in_sample_kernel, b=b, n_t=n_t),
        grid=(nc,),
        in_specs=[
            pl.BlockSpec((n_t, b, d), lambda c: (0, 0, 0)),
            pl.BlockSpec((b, tn), lambda c: (0, c)),
            pl.BlockSpec((b, tn), lambda c: (0, nc + c)),
            pl.BlockSpec((1, d), lambda c: (0, 0)),
            pl.BlockSpec((d, tn), lambda c: (0, c)),
            pl.BlockSpec((d, tn), lambda c: (0, nc + c)),
            pl.BlockSpec((d, tn), lambda c: (0, 2 * nc + c)),
            pl.BlockSpec((None, CONV_WIDTH, tn), lambda c: (layer, 0, c)),
        ],
        out_specs=[
            pl.BlockSpec((n_t * b, tn), lambda c: (0, c)),
            pl.BlockSpec((b, tn), lambda c: (0, c)),
            pl.BlockSpec((b, tn), lambda c: (0, c)),
        ],
        out_shape=[
            jax.ShapeDtypeStruct((n_t * b, d), BF16),
            jax.ShapeDtypeStruct((b, d), F32),
            jax.ShapeDtypeStruct((b, d), F32),
        ],
        scratch_shapes=[pltpu.VMEM((n_t * b, d), BF16)],
        compiler_params=_params(("arbitrary",)),
        name="conv_in_sample",
    )(x, state, state, g, w_in, w_in, w_in, cw)


def _proj_res_kernel(a_ref, w_ref, x_ref, o_ref):
    o_ref[...] = x_ref[...] + jnp.dot(a_ref[...], w_ref[...], preferred_element_type=F32)


def _conv_out(a, w, x):
    t, d = x.shape
    tm, tn = min(TILES["proj_tm"], t), min(TILES["proj_tn"], d)
    return pl.pallas_call(
        _proj_res_kernel,
        grid=(t // tm, d // tn),
        in_specs=[
            pl.BlockSpec((tm, d), lambda i, n: (i, 0)),
            pl.BlockSpec((d, tn), lambda i, n: (0, n)),
            pl.BlockSpec((tm, tn), lambda i, n: (i, n)),
        ],
        out_specs=pl.BlockSpec((tm, tn), lambda i, n: (i, n)),
        out_shape=jax.ShapeDtypeStruct((t, d), F32),
        compiler_params=_params(("arbitrary", "arbitrary")),
        name="conv_out",
    )(a, w, x)


def kernel(x_prompt, x_sample, state_pool, state_conv, norm_mix, norm_mlp, norm_final,
           w_pool, pool_scale, w_conv_in, conv_w, w_conv_out, w_up, w_down):
    bp, sp, d = x_prompt.shape
    bs, ss, _ = x_sample.shape

    w_pool_b = w_pool.astype(BF16)
    wup_b, wdn_b = (w_up, 0), (w_down, 0)
    wci_b = wco_b = None
    g_fin = norm_final.reshape(1, d)

    xp = x_prompt
    xs = x_sample.transpose(1, 0, 2)
    pool_p, conv_p, conv_s = [], [], []
    pool_s = None
    state_pool_t = state_pool.transpose(0, 2, 1, 3)
    for i in range(DEPTH):
        j = i // 2
        last = i == DEPTH - 1
        g_mix = norm_mix[i].reshape(1, d)
        if i % 2 == 0:
            sc = pool_scale[j].reshape(1, d)
            xp, st = _pool_prompt(xp, g_mix, w_pool_b, j, sc)
            pool_p.append(st)
            xs, pool_s = _pool_sample(xs, state_pool_t, g_mix, w_pool_b, j, sc, pool_s)
        else:
            a, st = _conv_in_prompt(xp, g_mix, wci_b, j, conv_w)
            conv_p.append(st)
            xp = _conv_out(a.reshape(bp * sp, d), wco_b, xp.reshape(bp * sp, d)).reshape(bp, sp, d)
            a, st0, st1 = _conv_in_sample(xs, state_conv[j].reshape(bs, CONV_STATE_LEN * d), g_mix,
                                          wci_b, j, conv_w)
            conv_s.append(jnp.stack([st0, st1], axis=1))
            xs = _conv_out(a, wco_b, xs.reshape(ss * bs, d)).reshape(ss, bs, d)
        g_mlp = norm_mlp[i].reshape(1, d)
        jobs = []
        if not last:
            if (i + 1) % 2 == 1:
                jobs += [(w_conv_in, (i + 1) // 2), (w_conv_out, (i + 1) // 2)]
            jobs += [(w_up, i + 1), (w_down, i + 1)]
        xs, emitted = _ffn(xs.reshape(ss * bs, d), g_mlp, g_fin, wup_b, wdn_b, final=last)
        xs = xs.reshape(ss, bs, d)
        if emitted:
            wup_b, wdn_b = emitted
        xp, cast = _ffn(xp.reshape(bp * sp, d), g_mlp, g_fin, wup_b, wdn_b, jobs, final=last)
        xp = xp.reshape(bp, sp, d)
        if not last:
            if (i + 1) % 2 == 1:
                wci_b, wco_b = cast[0], cast[1]
            wup_b, wdn_b = cast[-2], cast[-1]

    y_sample = xs.transpose(1, 0, 2)
    return (xp, y_sample, jnp.stack(pool_p), pool_s.transpose(0, 2, 1, 3), jnp.stack(conv_p), jnp.stack(conv_s))
```

```python
import functools

import jax
import jax.numpy as jnp
from jax import lax
from jax.experimental import pallas as pl
from jax.experimental.pallas import tpu as pltpu

DEPTH = 4
PAST_LEN = 16384
POOL_WINDOWS = (2, 4, 8, 16)
N_GROUPS = len(POOL_WINDOWS)
POOL_STATE_LEN = max(POOL_WINDOWS) - 1
CONV_WIDTH = 3
CONV_STATE_LEN = CONV_WIDTH - 1
RMS_EPS = 1e-6

F32 = jnp.float32
BF16 = jnp.bfloat16

V7X_VMEM_BYTES = 64 * 1024 * 1024
VMEM_LIMIT = V7X_VMEM_BYTES - 1024 * 1024
F32_SUBLANES = 8
BF16_SUBLANES = 16
HALO = 2 * F32_SUBLANES
assert POOL_STATE_LEN <= HALO and all(w & (w - 1) == 0 for w in POOL_WINDOWS)

TILES = dict(
    ffn_tm=1024, ffn_tf=512, ffn_tf_f32=256, ffn_tn=512, load_pc=32,
    pool_tt=512, pool_bb=16,
    conv_tm=1024, conv_tn=512,
    proj_tm=1024, proj_tn=1024,
)


def _params(sem):
    return pltpu.CompilerParams(dimension_semantics=sem, vmem_limit_bytes=VMEM_LIMIT)


def _rms(x, g):
    inv = lax.rsqrt(jnp.mean(x * x, axis=-1, keepdims=True) + RMS_EPS)
    return (x * inv) * g


ROW_CHUNK = 2 * BF16_SUBLANES


def _for_row_chunks(n_rows, fn):
    rc = min(ROW_CHUNK, n_rows)

    def body(r, carry):
        fn(pl.ds(pl.multiple_of(r * rc, rc), rc))
        return carry

    lax.fori_loop(0, n_rows // rc, body, 0)


LOAD_SLOTS = 4


def _load_rows_normed(src_rows, u_ref, g_ref, xbuf_ref, sem, n_chunks, pc):
    def copy(c):
        slot = c % LOAD_SLOTS
        return pltpu.make_async_copy(src_rows(c), xbuf_ref.at[slot], sem.at[slot])

    for c in range(min(LOAD_SLOTS - 1, n_chunks)):
        copy(c).start()

    def chunk_body(c, carry):
        @pl.when(c + LOAD_SLOTS - 1 < n_chunks)
        def _():
            copy(c + LOAD_SLOTS - 1).start()

        copy(c).wait()
        slot = c % LOAD_SLOTS
        for r in range(0, pc, ROW_CHUNK):
            rows = pl.ds(pl.multiple_of(c * pc + r, ROW_CHUNK), ROW_CHUNK)
            u_ref[rows, :] = _rms(xbuf_ref[slot, r:r + ROW_CHUNK, :], g_ref[...]).astype(BF16)
        return carry

    lax.fori_loop(0, n_chunks, chunk_body, 0)


def _ffn_kernel(*refs, n_cast, tm, tn, pc, final, emit_weights, multi_tile):
    x_hbm, xr_ref, xn_ref, g_ref, gf_ref, wup_ref, wdn_ref = refs[:7]
    cast_in = refs[7:7 + n_cast]
    o_hbm = refs[7 + n_cast]
    cast_out = refs[8 + n_cast:8 + 2 * n_cast]
    n_w = 2 if emit_weights else 0
    w_out = refs[8 + 2 * n_cast:8 + 2 * n_cast + n_w]
    acc_ref, u_ref, xbuf_ref, in_sem, out_sem = refs[8 + 2 * n_cast + n_w:]
    i, j = pl.program_id(0), pl.program_id(1)
    ni, nj = pl.num_programs(0), pl.num_programs(1)
    rr = xr_ref.shape[0]

    def tile_writeback(tile):
        rows = pl.ds(pl.multiple_of(tile * tm, tm), tm)
        return pltpu.make_async_copy(acc_ref, o_hbm.at[rows, :], out_sem.at[0])

    @pl.when(j == 0)
    def _():
        @pl.when(i == 0)
        def _():
            def src_rows(c):
                return x_hbm.at[pl.ds(pl.multiple_of(c * pc, pc), pc), :]

            _load_rows_normed(src_rows, u_ref.at[0], g_ref, xbuf_ref, in_sem, tm // pc, pc)

        @pl.when(i > 0)
        def _():
            tile_writeback(i - 1).wait()

        def zero_rows(rows):
            acc_ref[rows, :] = jnp.zeros((rows.size, acc_ref.shape[1]), F32)

        _for_row_chunks(tm, zero_rows)

    def weight(ref, out_ref, idx):
        w = ref[idx]
        if emit_weights:
            w = w.astype(BF16)
            out_ref[idx] = w
        return w

    h = jnp.dot(u_ref[i % 2], weight(wup_ref, w_out[0] if emit_weights else None, (slice(None), slice(None))),
                preferred_element_type=F32)
    h = jnp.maximum(h, 0.0)
    h = (h * h).astype(BF16)
    casts = list(zip(cast_in, cast_out))
    for k, n in enumerate(range(0, acc_ref.shape[1], tn)):
        wdn = weight(wdn_ref, w_out[1] if emit_weights else None, (slice(None), slice(n, n + tn)))
        acc_ref[:, n:n + tn] += jnp.dot(h, wdn, preferred_element_type=F32)
        if k < len(casts):
            casts[k][1][...] = casts[k][0][...].astype(BF16)
    for src, dst in casts[acc_ref.shape[1] // tn:]:
        dst[...] = src[...].astype(BF16)
    slab = pl.ds(pl.multiple_of(j * rr, rr), rr)
    acc_ref[slab, :] += xr_ref[...]
    if multi_tile:
        u_ref[(i + 1) % 2, slab, :] = _rms(xn_ref[...], g_ref[...]).astype(BF16)

    @pl.when(j == nj - 1)
    def _():
        if final:
            def final_rows(rows):
                acc_ref[rows, :] = _rms(acc_ref[rows, :], gf_ref[...])

            _for_row_chunks(tm, final_rows)
        tile_writeback(i).start()

        @pl.when(i == ni - 1)
        def _():
            tile_writeback(i).wait()


def _cast_specs(w, layer, n_steps, nj):
    _, r, c = w.shape
    reps = 1
    while (r * reps) % n_steps or (r * reps // n_steps) % BF16_SUBLANES:
        reps *= 2
        assert reps <= n_steps, (w.shape, n_steps)
    rb = r * reps // n_steps
    in_spec = pl.BlockSpec((None, rb, c), lambda i, j: (layer, (i * nj + j) // reps, 0))
    out_spec = pl.BlockSpec((rb, c), lambda i, j: ((i * nj + j) // reps, 0))
    return in_spec, out_spec, jax.ShapeDtypeStruct((r, c), BF16)


def _ffn(x, g, gf, wup, wdn, cast_jobs=(), *, final):
    t, d = x.shape
    emit_weights = isinstance(wup, tuple)
    if emit_weights:
        (wup, lu), (wdn, ld) = wup, wdn
        f = wup.shape[2]
        tf = min(TILES["ffn_tf_f32"], f)
        wup_spec = pl.BlockSpec((None, d, tf), lambda i, j: (lu, 0, j))
        wdn_spec = pl.BlockSpec((None, tf, d), lambda i, j: (ld, j, 0))
    else:
        f = wup.shape[1]
        tf = min(TILES["ffn_tf"], f)
        wup_spec = pl.BlockSpec((d, tf), lambda i, j: (0, j))
        wdn_spec = pl.BlockSpec((tf, d), lambda i, j: (j, 0))
    tm, tn = min(TILES["ffn_tm"], t), min(TILES["ffn_tn"], d)
    pc = min(TILES["load_pc"], tm)
    ni, nj = t // tm, f // tf
    rr = tm // nj
    assert rr % (BF16_SUBLANES if ni > 1 else F32_SUBLANES) == 0 and pc % ROW_CHUNK == 0, (tm, nj, pc)
    assert ni == 1 or not emit_weights
    cast = [_cast_specs(w, layer, ni * nj, nj) for w, layer in cast_jobs]
    w_specs, w_shapes = [], []
    if emit_weights:
        w_specs = [pl.BlockSpec((d, tf), lambda i, j: (0, j)), pl.BlockSpec((tf, d), lambda i, j: (j, 0))]
        w_shapes = [jax.ShapeDtypeStruct((d, f), BF16), jax.ShapeDtypeStruct((f, d), BF16)]
    outs = pl.pallas_call(
        functools.partial(_ffn_kernel, n_cast=len(cast), tm=tm, tn=tn, pc=pc, final=final,
                          emit_weights=emit_weights, multi_tile=ni > 1),
        grid=(ni, nj),
        in_specs=[
            pl.BlockSpec(memory_space=pl.ANY),
            pl.BlockSpec((rr, d), lambda i, j: (i * nj + j, 0)),
            pl.BlockSpec((rr, d), lambda i, j: (jnp.minimum(i + 1, ni - 1) * nj + j, 0)),
            pl.BlockSpec((1, d), lambda i, j: (0, 0)),
            pl.BlockSpec((1, d), lambda i, j: (0, 0)),
            wup_spec,
            wdn_spec,
        ] + [c[0] for c in cast],
        out_specs=[pl.BlockSpec(memory_space=pl.ANY)] + [c[1] for c in cast] + w_specs,
        out_shape=[jax.ShapeDtypeStruct((t, d), F32)] + [c[2] for c in cast] + w_shapes,
        scratch_shapes=[
            pltpu.VMEM((tm, d), F32),
            pltpu.VMEM((2, tm, d), BF16),
            pltpu.VMEM((LOAD_SLOTS, pc, d), F32),
            pltpu.SemaphoreType.DMA((LOAD_SLOTS,)),
            pltpu.SemaphoreType.DMA((1,)),
        ],
        compiler_params=_params(("arbitrary", "arbitrary")),
        name="ffn",
    )(x, x, x, g, gf, wup, wdn, *[w for w, _ in cast_jobs])
    return outs[0], list(outs[1:])


def _pool_prompt_kernel(x_ref, xh_ref, g_ref, w_ref, sc_ref, o_ref, st_ref, p_ref, *, tt):
    t = pl.program_id(1)
    g = g_ref[...]
    d = x_ref.shape[2]
    gd = d // N_GROUPS
    for r in range(0, tt, ROW_CHUNK):
        p_ref[HALO + r:HALO + r + ROW_CHUNK, :] = _rms(x_ref[0, r:r + ROW_CHUNK, :], g)

    @pl.when(t == 0)
    def _():
        p_ref[0:HALO, :] = jnp.zeros((HALO, d), F32)

    @pl.when(t > 0)
    def _():
        p_ref[0:HALO, :] = _rms(xh_ref[0], g)

    pos = t * tt + lax.broadcasted_iota(jnp.int32, (tt, 1), 0)
    for gi, w in enumerate(POOL_WINDOWS):
        sl = slice(gi * gd, (gi + 1) * gd)
        u = p_ref[HALO:HALO + tt, sl]
        s = p_ref[0:HALO + tt, sl]
        k = 1
        while k < w:
            s = s + pltpu.roll(s, k, 0)
            k *= 2
        s = s[HALO:HALO + tt]
        cnt = jnp.minimum(pos + 1, w).astype(F32)
        diff = (s / cnt - u).astype(BF16)
        y = jnp.dot(diff, w_ref[gi], preferred_element_type=F32) * sc_ref[:, sl]
        o_ref[0, :, sl] = x_ref[0, :, sl] + y
    st_ref[0] = p_ref[HALO + tt - POOL_STATE_LEN:HALO + tt, :]


def _pool_prompt(x, g, w, layer, sc):
    b, s, d = x.shape
    gd = d // N_GROUPS
    tt = min(TILES["pool_tt"], s)
    hb = tt // HALO
    return pl.pallas_call(
        functools.partial(_pool_prompt_kernel, tt=tt),
        grid=(b, s // tt),
        in_specs=[
            pl.BlockSpec((1, tt, d), lambda i, t: (i, t, 0)),
            pl.BlockSpec((1, HALO, d), lambda i, t: (i, jnp.maximum(t * hb - 1, 0), 0)),
            pl.BlockSpec((1, d), lambda i, t: (0, 0)),
            pl.BlockSpec((None, N_GROUPS, gd, gd), lambda i, t: (layer, 0, 0, 0), pipeline_mode=pl.Buffered(1)),
            pl.BlockSpec((1, d), lambda i, t: (0, 0)),
        ],
        out_specs=[
            pl.BlockSpec((1, tt, d), lambda i, t: (i, t, 0)),
            pl.BlockSpec((1, POOL_STATE_LEN, d), lambda i, t: (i, 0, 0)),
        ],
        out_shape=[
            jax.ShapeDtypeStruct((b, s, d), F32),
            jax.ShapeDtypeStruct((b, POOL_STATE_LEN, d), F32),
        ],
        scratch_shapes=[pltpu.VMEM((HALO + tt, d), F32)],
        compiler_params=_params(("arbitrary", "arbitrary")),
        name="pool_prompt",
    )(x, x, g, w, sc)


def _pool_sample_kernel(x_ref, s_ref, g_ref, w_ref, sc_ref, *rest, bb, n_t, layer, first):
    o_ref, so_all_ref, u_ref, d_ref = rest[-4:]
    if first:
        so_ref = so_all_ref.at[layer]
        for other in range(so_all_ref.shape[0]):
            if other != layer:
                so_all_ref[other] = jnp.zeros(so_all_ref.shape[1:], F32)
    else:
        so_ref = so_all_ref
    d = u_ref.shape[2]
    gd = d // N_GROUPS
    g = g_ref[...]
    for t in range(n_t):
        u_ref[t] = _rms(x_ref[t], g)

    def padded_row(idx, sl):
        if idx < POOL_STATE_LEN:
            return s_ref[idx, :, sl]
        return u_ref[idx - POOL_STATE_LEN, :, sl]

    for gi, w in enumerate(POOL_WINDOWS):
        sl = slice(gi * gd, (gi + 1) * gd)
        for t in range(n_t):
            s = padded_row(POOL_STATE_LEN + t, sl)
            for k in range(1, w):
                s = s + padded_row(POOL_STATE_LEN + t - k, sl)
            cnt = float(min(PAST_LEN + t + 1, w))
            d_ref[t * bb:(t + 1) * bb, sl] = (s / cnt - u_ref[t, :, sl]).astype(BF16)
        y = jnp.dot(d_ref[:, sl], w_ref[gi], preferred_element_type=F32) * sc_ref[:, sl]
        for t in range(n_t):
            o_ref[t, :, sl] = x_ref[t, :, sl] + y[t * bb:(t + 1) * bb]

    keep = POOL_STATE_LEN - n_t
    for k in range(keep):
        so_ref[k] = s_ref[n_t + k]
    for t in range(n_t):
        so_ref[keep + t] = u_ref[t]


def _pool_sample(x, state, g, w, layer, sc, states_out):
    n_t, b, d = x.shape
    gd = d // N_GROUPS
    assert n_t <= POOL_STATE_LEN
    bb = min(TILES["pool_bb"], b)
    prev = [] if states_out is None else [states_out]
    n_layers = w.shape[0]
    if prev:
        so_spec = pl.BlockSpec((None, POOL_STATE_LEN, bb, d), lambda i: (layer, 0, i, 0))
    else:
        so_spec = pl.BlockSpec((n_layers, POOL_STATE_LEN, bb, d), lambda i: (0, 0, i, 0))
    return pl.pallas_call(
        functools.partial(_pool_sample_kernel, bb=bb, n_t=n_t, layer=layer, first=not prev),
        grid=(b // bb,),
        in_specs=[
            pl.BlockSpec((n_t, bb, d), lambda i: (0, i, 0)),
            pl.BlockSpec((None, POOL_STATE_LEN, bb, d), lambda i: (layer, 0, i, 0)),
            pl.BlockSpec((1, d), lambda i: (0, 0)),
            pl.BlockSpec((None, N_GROUPS, gd, gd), lambda i: (layer, 0, 0, 0), pipeline_mode=pl.Buffered(1)),
            pl.BlockSpec((1, d), lambda i: (0, 0)),
        ] + [pl.BlockSpec(memory_space=pl.ANY) for _ in prev],
        out_specs=[
            pl.BlockSpec((n_t, bb, d), lambda i: (0, i, 0)),
            so_spec,
        ],
        out_shape=[
            jax.ShapeDtypeStruct((n_t, b, d), F32),
            jax.ShapeDtypeStruct((n_layers, POOL_STATE_LEN, b, d), F32),
        ],
        input_output_aliases={5: 1} if prev else {},
        scratch_shapes=[pltpu.VMEM((n_t, bb, d), F32), pltpu.VMEM((n_t * bb, d), BF16)],
        compiler_params=_params(("arbitrary",)),
        name="pool_sample",
    )(x, state, g, w, sc, *prev)


def _conv_in_prompt_kernel(x_hbm, xn_ref, g_ref, wb_ref, wc_ref, wh_ref, cw_ref, o_ref, st_ref,
                           u_ref, v_ref, carry_ref, xbuf_ref, in_sem, *, tm, pc):
    t = pl.program_id(1)
    c = pl.program_id(2)
    tile = pl.program_id(0) * pl.num_programs(1) + t
    sub = F32_SUBLANES

    @pl.when((tile == 0) & (c == 0))
    def _():
        def src_rows(k):
            return x_hbm.at[0, pl.ds(pl.multiple_of(k * pc, pc), pc), :]

        _load_rows_normed(src_rows, u_ref.at[0], g_ref, xbuf_ref, in_sem, tm // pc, pc)

    u = u_ref[tile % 2]
    zc = jnp.dot(u, wc_ref[...], preferred_element_type=F32)
    zh = jnp.dot(u, wh_ref[...], preferred_element_type=F32)
    v_ref[sub:sub + tm, :] = zc * zh
    v_ref[0:sub, :] = jnp.where(t > 0, carry_ref[c], 0.0)

    conv = cw_ref[CONV_WIDTH - 1:CONV_WIDTH, :] * v_ref[sub:sub + tm, :]
    for k in range(CONV_WIDTH - 1):
        off = sub - (CONV_WIDTH - 1) + k
        conv = conv + cw_ref[k:k + 1, :] * v_ref[off:off + tm, :]
    zb = jnp.dot(u, wb_ref[...], preferred_element_type=F32)
    o_ref[0] = (zb * conv).astype(BF16)
    carry_ref[c] = v_ref[tm:tm + sub, :]

    rn = xn_ref.shape[1]
    rc = min(ROW_CHUNK, rn)
    for r in range(0, rn, rc):
        rows = pl.ds(pl.multiple_of(c * rn + r, rc), rc)
        u_ref[(tile + 1) % 2, rows, :] = _rms(xn_ref[0, r:r + rc, :], g_ref[...]).astype(BF16)

    @pl.when(t == pl.num_programs(1) - 1)
    def _():
        st_ref[0, c] = v_ref[sub + tm - CONV_STATE_LEN:sub + tm, :]


def _conv_in_prompt(x, g, w_in, layer, cw):
    b, s, d = x.shape
    tm, tn = min(TILES["conv_tm"], s), min(TILES["conv_tn"], d)
    pc = min(TILES["load_pc"], tm)
    nc = d // tn
    nt = s // tm
    rn = tm // nc
    assert rn % BF16_SUBLANES == 0, (tm, nc)

    def next_slab(i, t, c):
        nxt = jnp.minimum(i * nt + t + 1, b * nt - 1)
        return (nxt // nt, (nxt % nt) * nc + c, 0)

    gated, st = pl.pallas_call(
        functools.partial(_conv_in_prompt_kernel, tm=tm, pc=pc),
        grid=(b, nt, nc),
        in_specs=[
            pl.BlockSpec(memory_space=pl.ANY),
            pl.BlockSpec((1, rn, d), next_slab),
            pl.BlockSpec((1, d), lambda i, t, c: (0, 0)),
            pl.BlockSpec((d, tn), lambda i, t, c: (0, c)),
            pl.BlockSpec((d, tn), lambda i, t, c: (0, nc + c)),
            pl.BlockSpec((d, tn), lambda i, t, c: (0, 2 * nc + c)),
            pl.BlockSpec((None, CONV_WIDTH, tn), lambda i, t, c: (layer, 0, c)),
        ],
        out_specs=[
            pl.BlockSpec((1, tm, tn), lambda i, t, c: (i, t, c)),
            pl.BlockSpec((1, nc, CONV_STATE_LEN, tn), lambda i, t, c: (i, 0, 0, 0)),
        ],
        out_shape=[
            jax.ShapeDtypeStruct((b, s, d), BF16),
            jax.ShapeDtypeStruct((b, nc, CONV_STATE_LEN, tn), F32),
        ],
        scratch_shapes=[
            pltpu.VMEM((2, tm, d), BF16),
            pltpu.VMEM((F32_SUBLANES + tm, tn), F32),
            pltpu.VMEM((nc, F32_SUBLANES, tn), F32),
            pltpu.VMEM((LOAD_SLOTS, pc, d), F32),
            pltpu.SemaphoreType.DMA((LOAD_SLOTS,)),
        ],
        compiler_params=_params(("arbitrary", "arbitrary", "arbitrary")),
        name="conv_in_prompt",
    )(x, x, g, w_in, w_in, w_in, cw)
    return gated, st.transpose(0, 2, 1, 3).reshape(b, CONV_STATE_LEN, d)


def _conv_in_sample_kernel(x_ref, s0_ref, s1_ref, g_ref, wb_ref, wc_ref, wh_ref, cw_ref,
                           o_ref, st0_ref, st1_ref, u_ref, *, b, n_t):
    c = pl.program_id(0)

    @pl.when(c == 0)
    def _():
        for t in range(n_t):
            u_ref[t * b:(t + 1) * b, :] = _rms(x_ref[t], g_ref[...]).astype(BF16)

    u = u_ref[...]
    zc = jnp.dot(u, wc_ref[...], preferred_element_type=F32)
    zh = jnp.dot(u, wh_ref[...], preferred_element_type=F32)
    zb = jnp.dot(u, wb_ref[...], preferred_element_type=F32)
    v = zc * zh
    rows = [s0_ref[...], s1_ref[...]] + [v[t * b:(t + 1) * b] for t in range(n_t)]
    for t in range(n_t):
        conv = cw_ref[0:1, :] * rows[t]
        for k in range(1, CONV_WIDTH):
            conv = conv + cw_ref[k:k + 1, :] * rows[t + k]
        o_ref[t * b:(t + 1) * b, :] = (zb[t * b:(t + 1) * b] * conv).astype(BF16)
    st0_ref[...] = rows[n_t]
    st1_ref[...] = rows[n_t + 1]


def _conv_in_sample(x, state, g, w_in, layer, cw):
    n_t, b, d = x.shape
    tn = min(TILES["conv_tn"], d)
    nc = d // tn
    assert CONV_STATE_LEN == 2
    return pl.pallas_call(
        functools.partial(_conv_in_sample_kernel, b=b, n_t=n_t),
        grid=(nc,),
        in_specs=[
            pl.BlockSpec((n_t, b, d), lambda c: (0, 0, 0)),
            pl.BlockSpec((b, tn), lambda c: (0, c)),
            pl.BlockSpec((b, tn), lambda c: (0, nc + c)),
            pl.BlockSpec((1, d), lambda c: (0, 0)),
            pl.BlockSpec((d, tn), lambda c: (0, c)),
            pl.BlockSpec((d, tn), lambda c: (0, nc + c)),
            pl.BlockSpec((d, tn), lambda c: (0, 2 * nc + c)),
            pl.BlockSpec((None, CONV_WIDTH, tn), lambda c: (layer, 0, c)),
        ],
        out_specs=[
            pl.BlockSpec((n_t * b, tn), lambda c: (0, c)),
            pl.BlockSpec((b, tn), lambda c: (0, c)),
            pl.BlockSpec((b, tn), lambda c: (0, c)),
        ],
        out_shape=[
            jax.ShapeDtypeStruct((n_t * b, d), BF16),
            jax.ShapeDtypeStruct((b, d), F32),
            jax.ShapeDtypeStruct((b, d), F32),
        ],
        scratch_shapes=[pltpu.VMEM((n_t * b, d), BF16)],
        compiler_params=_params(("arbitrary",)),
        name="conv_in_sample",
    )(x, state, state, g, w_in, w_in, w_in, cw)


def _proj_res_kernel(a_ref, w_ref, x_ref, o_ref):
    o_ref[...] = x_ref[...] + jnp.dot(a_ref[...], w_ref[...], preferred_element_type=F32)


def _conv_out(a, w, x):
    t, d = x.shape
    tm, tn = min(TILES["proj_tm"], t), min(TILES["proj_tn"], d)
    return pl.pallas_call(
        _proj_res_kernel,
        grid=(t // tm, d // tn),
        in_specs=[
            pl.BlockSpec((tm, d), lambda i, n: (i, 0)),
            pl.BlockSpec((d, tn), lambda i, n: (0, n)),
            pl.BlockSpec((tm, tn), lambda i, n: (i, n)),
        ],
        out_specs=pl.BlockSpec((tm, tn), lambda i, n: (i, n)),
        out_shape=jax.ShapeDtypeStruct((t, d), F32),
        compiler_params=_params(("arbitrary", "arbitrary")),
        name="conv_out",
    )(a, w, x)


def kernel(x_prompt, x_sample, state_pool, state_conv, norm_mix, norm_mlp, norm_final,
           w_pool, pool_scale, w_conv_in, conv_w, w_conv_out, w_up, w_down):
    bp, sp, d = x_prompt.shape
    bs, ss, _ = x_sample.shape

    w_pool_b = w_pool.astype(BF16)
    wup_b, wdn_b = (w_up, 0), (w_down, 0)
    wci_b = wco_b = None
    g_fin = norm_final.reshape(1, d)

    xp = x_prompt
    xs = x_sample.transpose(1, 0, 2)
    pool_p, conv_p, conv_s = [], [], []
    pool_s = None
    state_pool_t = state_pool.transpose(0, 2, 1, 3)
    for i in range(DEPTH):
        j = i // 2
        last = i == DEPTH - 1
        g_mix = norm_mix[i].reshape(1, d)
        if i % 2 == 0:
            sc = pool_scale[j].reshape(1, d)
            xp, st = _pool_prompt(xp, g_mix, w_pool_b, j, sc)
            pool_p.append(st)
            xs, pool_s = _pool_sample(xs, state_pool_t, g_mix, w_pool_b, j, sc, pool_s)
        else:
            a, st = _conv_in_prompt(xp, g_mix, wci_b, j, conv_w)
            conv_p.append(st)
            xp = _conv_out(a.reshape(bp * sp, d), wco_b, xp.reshape(bp * sp, d)).reshape(bp, sp, d)
            a, st0, st1 = _conv_in_sample(xs, state_conv[j].reshape(bs, CONV_STATE_LEN * d), g_mix,
                                          wci_b, j, conv_w)
            conv_s.append(jnp.stack([st0, st1], axis=1))
            xs = _conv_out(a, wco_b, xs.reshape(ss * bs, d)).reshape(ss, bs, d)
        g_mlp = norm_mlp[i].reshape(1, d)
        jobs = []
        if not last:
            if (i + 1) % 2 == 1:
                jobs += [(w_conv_in, (i + 1) // 2), (w_conv_out, (i + 1) // 2)]
            jobs += [(w_up, i + 1), (w_down, i + 1)]
        xs, emitted = _ffn(xs.reshape(ss * bs, d), g_mlp, g_fin, wup_b, wdn_b, final=last)
        xs = xs.reshape(ss, bs, d)
        if emitted:
            wup_b, wdn_b = emitted
        xp, cast = _ffn(xp.reshape(bp * sp, d), g_mlp, g_fin, wup_b, wdn_b, jobs, final=last)
        xp = xp.reshape(bp, sp, d)
        if not last:
            if (i + 1) % 2 == 1:
                wci_b, wco_b = cast[0], cast[1]
            wup_b, wdn_b = cast[-2], cast[-1]

    y_sample = xs.transpose(1, 0, 2)
    return (xp, y_sample, jnp.stack(pool_p), pool_s.transpose(0, 2, 1, 3), jnp.stack(conv_p), jnp.stack(conv_s))
```

```python
import functools

import jax
import jax.numpy as jnp
from jax import lax
from jax.experimental import pallas as pl
from jax.experimental.pallas import tpu as pltpu

DEPTH = 4
PAST_LEN = 16384
POOL_WINDOWS = (2, 4, 8, 16)
N_GROUPS = len(POOL_WINDOWS)
POOL_STATE_LEN = max(POOL_WINDOWS) - 1
CONV_WIDTH = 3
CONV_STATE_LEN = CONV_WIDTH - 1
RMS_EPS = 1e-6

F32 = jnp.float32
BF16 = jnp.bfloat16

V7X_VMEM_BYTES = 64 * 1024 * 1024
VMEM_LIMIT = V7X_VMEM_BYTES - 1024 * 1024
F32_SUBLANES = 8
BF16_SUBLANES = 16
HALO = 2 * F32_SUBLANES
assert POOL_STATE_LEN <= HALO and all(w & (w - 1) == 0 for w in POOL_WINDOWS)

TILES = dict(
    ffn_tm=1024, ffn_tf=512, ffn_tf_f32=256, ffn_tn=512, load_pc=32,
    pool_tt=512, pool_bb=16,
    conv_tm=1024, conv_tn=512,
    proj_tm=1024, proj_tn=1024,
)


def _params(sem):
    return pltpu.CompilerParams(dimension_semantics=sem, vmem_limit_bytes=VMEM_LIMIT)


def _rms(x, g):
    inv = lax.rsqrt(jnp.mean(x * x, axis=-1, keepdims=True) + RMS_EPS)
    return (x * inv) * g


ROW_CHUNK = 2 * BF16_SUBLANES


def _for_row_chunks(n_rows, fn):
    rc = min(ROW_CHUNK, n_rows)

    def body(r, carry):
        fn(pl.ds(pl.multiple_of(r * rc, rc), rc))
        return carry

    lax.fori_loop(0, n_rows // rc, body, 0)


LOAD_SLOTS = 3


def _load_rows_normed(src_rows, u_ref, g_ref, xbuf_ref, sem, n_chunks, pc):
    def copy(c):
        slot = c % LOAD_SLOTS
        return pltpu.make_async_copy(src_rows(c), xbuf_ref.at[slot], sem.at[slot])

    for c in range(min(LOAD_SLOTS - 1, n_chunks)):
        copy(c).start()

    def chunk_body(c, carry):
        @pl.when(c + LOAD_SLOTS - 1 < n_chunks)
        def _():
            copy(c + LOAD_SLOTS - 1).start()

        copy(c).wait()
        slot = c % LOAD_SLOTS
        for r in range(0, pc, ROW_CHUNK):
            rows = pl.ds(pl.multiple_of(c * pc + r, ROW_CHUNK), ROW_CHUNK)
            u_ref[rows, :] = _rms(xbuf_ref[slot, r:r + ROW_CHUNK, :], g_ref[...]).astype(BF16)
        return carry

    lax.fori_loop(0, n_chunks, chunk_body, 0)


def _ffn_kernel(*refs, n_cast, tm, tn, pc, final, emit_weights, multi_tile):
    x_hbm, xr_ref, xn_ref, g_ref, gf_ref, wup_ref, wdn_ref = refs[:7]
    cast_in = refs[7:7 + n_cast]
    o_hbm = refs[7 + n_cast]
    cast_out = refs[8 + n_cast:8 + 2 * n_cast]
    n_w = 2 if emit_weights else 0
    w_out = refs[8 + 2 * n_cast:8 + 2 * n_cast + n_w]
    acc_ref, u_ref, xbuf_ref, in_sem, out_sem = refs[8 + 2 * n_cast + n_w:]
    i, j = pl.program_id(0), pl.program_id(1)
    ni, nj = pl.num_programs(0), pl.num_programs(1)
    rr = xr_ref.shape[0]

    def tile_writeback(tile):
        rows = pl.ds(pl.multiple_of(tile * tm, tm), tm)
        return pltpu.make_async_copy(acc_ref, o_hbm.at[rows, :], out_sem.at[0])

    @pl.when(j == 0)
    def _():
        @pl.when(i == 0)
        def _():
            def src_rows(c):
                return x_hbm.at[pl.ds(pl.multiple_of(c * pc, pc), pc), :]

            _load_rows_normed(src_rows, u_ref.at[0], g_ref, xbuf_ref, in_sem, tm // pc, pc)

        @pl.when(i > 0)
        def _():
            tile_writeback(i - 1).wait()

        def zero_rows(rows):
            acc_ref[rows, :] = jnp.zeros((rows.size, acc_ref.shape[1]), F32)

        _for_row_chunks(tm, zero_rows)

    def weight(ref, out_ref, idx):
        w = ref[idx]
        if emit_weights:
            w = w.astype(BF16)
            out_ref[idx] = w
        return w

    h = jnp.dot(u_ref[i % 2], weight(wup_ref, w_out[0] if emit_weights else None, (slice(None), slice(None))),
                preferred_element_type=F32)
    h = jnp.maximum(h, 0.0)
    h = (h * h).astype(BF16)
    casts = list(zip(cast_in, cast_out))
    for k, n in enumerate(range(0, acc_ref.shape[1], tn)):
        wdn = weight(wdn_ref, w_out[1] if emit_weights else None, (slice(None), slice(n, n + tn)))
        acc_ref[:, n:n + tn] += jnp.dot(h, wdn, preferred_element_type=F32)
        if k < len(casts):
            casts[k][1][...] = casts[k][0][...].astype(BF16)
    for src, dst in casts[acc_ref.shape[1] // tn:]:
        dst[...] = src[...].astype(BF16)
    slab = pl.ds(pl.multiple_of(j * rr, rr), rr)
    acc_ref[slab, :] += xr_ref[...]
    if multi_tile:
        u_ref[(i + 1) % 2, slab, :] = _rms(xn_ref[...], g_ref[...]).astype(BF16)

    @pl.when(j == nj - 1)
    def _():
        if final:
            def final_rows(rows):
                acc_ref[rows, :] = _rms(acc_ref[rows, :], gf_ref[...])

            _for_row_chunks(tm, final_rows)
        tile_writeback(i).start()

        @pl.when(i == ni - 1)
        def _():
            tile_writeback(i).wait()


def _cast_specs(w, layer, n_steps, nj):
    _, r, c = w.shape
    reps = 1
    while (r * reps) % n_steps or (r * reps // n_steps) % BF16_SUBLANES:
        reps *= 2
        assert reps <= n_steps, (w.shape, n_steps)
    rb = r * reps // n_steps
    in_spec = pl.BlockSpec((None, rb, c), lambda i, j: (layer, (i * nj + j) // reps, 0))
    out_spec = pl.BlockSpec((rb, c), lambda i, j: ((i * nj + j) // reps, 0))
    return in_spec, out_spec, jax.ShapeDtypeStruct((r, c), BF16)


def _ffn(x, g, gf, wup, wdn, cast_jobs=(), *, final):
    t, d = x.shape
    emit_weights = isinstance(wup, tuple)
    if emit_weights:
        (wup, lu), (wdn, ld) = wup, wdn
        f = wup.shape[2]
        tf = min(TILES["ffn_tf_f32"], f)
        wup_spec = pl.BlockSpec((None, d, tf), lambda i, j: (lu, 0, j))
        wdn_spec = pl.BlockSpec((None, tf, d), lambda i, j: (ld, j, 0))
    else:
        f = wup.shape[1]
        tf = min(TILES["ffn_tf"], f)
        wup_spec = pl.BlockSpec((d, tf), lambda i, j: (0, j))
        wdn_spec = pl.BlockSpec((tf, d), lambda i, j: (j, 0))
    tm, tn = min(TILES["ffn_tm"], t), min(TILES["ffn_tn"], d)
    pc = min(TILES["load_pc"], tm)
    ni, nj = t // tm, f // tf
    rr = tm // nj
    assert rr % (BF16_SUBLANES if ni > 1 else F32_SUBLANES) == 0 and pc % ROW_CHUNK == 0, (tm, nj, pc)
    assert ni == 1 or not emit_weights
    cast = [_cast_specs(w, layer, ni * nj, nj) for w, layer in cast_jobs]
    w_specs, w_shapes = [], []
    if emit_weights:
        w_specs = [pl.BlockSpec((d, tf), lambda i, j: (0, j)), pl.BlockSpec((tf, d), lambda i, j: (j, 0))]
        w_shapes = [jax.ShapeDtypeStruct((d, f), BF16), jax.ShapeDtypeStruct((f, d), BF16)]
    outs = pl.pallas_call(
        functools.partial(_ffn_kernel, n_cast=len(cast), tm=tm, tn=tn, pc=pc, final=final,
                          emit_weights=emit_weights, multi_tile=ni > 1),
        grid=(ni, nj),
        in_specs=[
            pl.BlockSpec(memory_space=pl.ANY),
            pl.BlockSpec((rr, d), lambda i, j: (i * nj + j, 0)),
            pl.BlockSpec((rr, d), lambda i, j: (jnp.minimum(i + 1, ni - 1) * nj + j, 0)),
            pl.BlockSpec((1, d), lambda i, j: (0, 0)),
            pl.BlockSpec((1, d), lambda i, j: (0, 0)),
            wup_spec,
            wdn_spec,
        ] + [c[0] for c in cast],
        out_specs=[pl.BlockSpec(memory_space=pl.ANY)] + [c[1] for c in cast] + w_specs,
        out_shape=[jax.ShapeDtypeStruct((t, d), F32)] + [c[2] for c in cast] + w_shapes,
        scratch_shapes=[
            pltpu.VMEM((tm, d), F32),
            pltpu.VMEM((2, tm, d), BF16),
            pltpu.VMEM((LOAD_SLOTS, pc, d), F32),
            pltpu.SemaphoreType.DMA((LOAD_SLOTS,)),
            pltpu.SemaphoreType.DMA((1,)),
        ],
        compiler_params=_params(("arbitrary", "arbitrary")),
        name="ffn",
    )(x, x, x, g, gf, wup, wdn, *[w for w, _ in cast_jobs])
    return outs[0], list(outs[1:])


def _pool_prompt_kernel(x_ref, xh_ref, g_ref, w_ref, sc_ref, o_ref, st_ref, p_ref, *, tt):
    t = pl.program_id(1)
    g = g_ref[...]
    d = x_ref.shape[2]
    gd = d // N_GROUPS
    for r in range(0, tt, ROW_CHUNK):
        p_ref[HALO + r:HALO + r + ROW_CHUNK, :] = _rms(x_ref[0, r:r + ROW_CHUNK, :], g)

    @pl.when(t == 0)
    def _():
        p_ref[0:HALO, :] = jnp.zeros((HALO, d), F32)

    @pl.when(t > 0)
    def _():
        p_ref[0:HALO, :] = _rms(xh_ref[0], g)

    pos = t * tt + lax.broadcasted_iota(jnp.int32, (tt, 1), 0)
    for gi, w in enumerate(POOL_WINDOWS):
        sl = slice(gi * gd, (gi + 1) * gd)
        u = p_ref[HALO:HALO + tt, sl]
        s = p_ref[0:HALO + tt, sl]
        k = 1
        while k < w:
            s = s + pltpu.roll(s, k, 0)
            k *= 2
        s = s[HALO:HALO + tt]
        cnt = jnp.minimum(pos + 1, w).astype(F32)
        diff = (s / cnt - u).astype(BF16)
        y = jnp.dot(diff, w_ref[gi], preferred_element_type=F32) * sc_ref[:, sl]
        o_ref[0, :, sl] = x_ref[0, :, sl] + y
    st_ref[0] = p_ref[HALO + tt - POOL_STATE_LEN:HALO + tt, :]


def _pool_prompt(x, g, w, layer, sc):
    b, s, d = x.shape
    gd = d // N_GROUPS
    tt = min(TILES["pool_tt"], s)
    hb = tt // HALO
    return pl.pallas_call(
        functools.partial(_pool_prompt_kernel, tt=tt),
        grid=(b, s // tt),
        in_specs=[
            pl.BlockSpec((1, tt, d), lambda i, t: (i, t, 0)),
            pl.BlockSpec((1, HALO, d), lambda i, t: (i, jnp.maximum(t * hb - 1, 0), 0)),
            pl.BlockSpec((1, d), lambda i, t: (0, 0)),
            pl.BlockSpec((None, N_GROUPS, gd, gd), lambda i, t: (layer, 0, 0, 0), pipeline_mode=pl.Buffered(1)),
            pl.BlockSpec((1, d), lambda i, t: (0, 0)),
        ],
        out_specs=[
            pl.BlockSpec((1, tt, d), lambda i, t: (i, t, 0)),
            pl.BlockSpec((1, POOL_STATE_LEN, d), lambda i, t: (i, 0, 0)),
        ],
        out_shape=[
            jax.ShapeDtypeStruct((b, s, d), F32),
            jax.ShapeDtypeStruct((b, POOL_STATE_LEN, d), F32),
        ],
        scratch_shapes=[pltpu.VMEM((HALO + tt, d), F32)],
        compiler_params=_params(("arbitrary", "arbitrary")),
        name="pool_prompt",
    )(x, x, g, w, sc)


def _pool_sample_kernel(x_ref, s_ref, g_ref, w_ref, sc_ref, *rest, bb, n_t, layer, first):
    o_ref, so_all_ref, u_ref, d_ref = rest[-4:]
    if first:
        so_ref = so_all_ref.at[layer]
        for other in range(so_all_ref.shape[0]):
            if other != layer:
                so_all_ref[other] = jnp.zeros(so_all_ref.shape[1:], F32)
    else:
        so_ref = so_all_ref
    d = u_ref.shape[2]
    gd = d // N_GROUPS
    g = g_ref[...]
    for t in range(n_t):
        u_ref[t] = _rms(x_ref[t], g)

    def padded_row(idx, sl):
        if idx < POOL_STATE_LEN:
            return s_ref[idx, :, sl]
        return u_ref[idx - POOL_STATE_LEN, :, sl]

    for gi, w in enumerate(POOL_WINDOWS):
        sl = slice(gi * gd, (gi + 1) * gd)
        for t in range(n_t):
            s = padded_row(POOL_STATE_LEN + t, sl)
            for k in range(1, w):
                s = s + padded_row(POOL_STATE_LEN + t - k, sl)
            cnt = float(min(PAST_LEN + t + 1, w))
            d_ref[t * bb:(t + 1) * bb, sl] = (s / cnt - u_ref[t, :, sl]).astype(BF16)
        y = jnp.dot(d_ref[:, sl], w_ref[gi], preferred_element_type=F32) * sc_ref[:, sl]
        for t in range(n_t):
            o_ref[t, :, sl] = x_ref[t, :, sl] + y[t * bb:(t + 1) * bb]

    keep = POOL_STATE_LEN - n_t
    for k in range(keep):
        so_ref[k] = s_ref[n_t + k]
    for t in range(n_t):
        so_ref[keep + t] = u_ref[t]


def _pool_sample(x, state, g, w, layer, sc, states_out):
    n_t, b, d = x.shape
    gd = d // N_GROUPS
    assert n_t <= POOL_STATE_LEN
    bb = min(TILES["pool_bb"], b)
    prev = [] if states_out is None else [states_out]
    n_layers = w.shape[0]
    if prev:
        so_spec = pl.BlockSpec((None, POOL_STATE_LEN, bb, d), lambda i: (layer, 0, i, 0))
    else:
        so_spec = pl.BlockSpec((n_layers, POOL_STATE_LEN, bb, d), lambda i: (0, 0, i, 0))
    return pl.pallas_call(
        functools.partial(_pool_sample_kernel, bb=bb, n_t=n_t, layer=layer, first=not prev),
        grid=(b // bb,),
        in_specs=[
            pl.BlockSpec((n_t, bb, d), lambda i: (0, i, 0)),
            pl.BlockSpec((None, POOL_STATE_LEN, bb, d), lambda i: (layer, 0, i, 0)),
            pl.BlockSpec((1, d), lambda i: (0, 0)),
            pl.BlockSpec((None, N_GROUPS, gd, gd), lambda i: (layer, 0, 0, 0), pipeline_mode=pl.Buffered(1)),
            pl.BlockSpec((1, d), lambda i: (0, 0)),
        ] + [pl.BlockSpec(memory_space=pl.ANY) for _ in prev],
        out_specs=[
            pl.BlockSpec((n_t, bb, d), lambda i: (0, i, 0)),
            so_spec,
        ],
        out_shape=[
            jax.ShapeDtypeStruct((n_t, b, d), F32),
            jax.ShapeDtypeStruct((n_layers, POOL_STATE_LEN, b, d), F32),
        ],
        input_output_aliases={5: 1} if prev else {},
        scratch_shapes=[pltpu.VMEM((n_t, bb, d), F32), pltpu.VMEM((n_t * bb, d), BF16)],
        compiler_params=_params(("arbitrary",)),
        name="pool_sample",
    )(x, state, g, w, sc, *prev)


def _conv_in_prompt_kernel(x_hbm, xn_ref, g_ref, wb_ref, wc_ref, wh_ref, cw_ref, o_ref, st_ref,
                           u_ref, v_ref, carry_ref, xbuf_ref, in_sem, *, tm, pc):
    t = pl.program_id(1)
    c = pl.program_id(2)
    tile = pl.program_id(0) * pl.num_programs(1) + t
    sub = F32_SUBLANES

    @pl.when((tile == 0) & (c == 0))
    def _():
        def src_rows(k):
            return x_hbm.at[0, pl.ds(pl.multiple_of(k * pc, pc), pc), :]

        _load_rows_normed(src_rows, u_ref.at[0], g_ref, xbuf_ref, in_sem, tm // pc, pc)

    u = u_ref[tile % 2]
    zc = jnp.dot(u, wc_ref[...], preferred_element_type=F32)
    zh = jnp.dot(u, wh_ref[...], preferred_element_type=F32)
    v_ref[sub:sub + tm, :] = zc * zh
    v_ref[0:sub, :] = jnp.where(t > 0, carry_ref[c], 0.0)

    conv = cw_ref[CONV_WIDTH - 1:CONV_WIDTH, :] * v_ref[sub:sub + tm, :]
    for k in range(CONV_WIDTH - 1):
        off = sub - (CONV_WIDTH - 1) + k
        conv = conv + cw_ref[k:k + 1, :] * v_ref[off:off + tm, :]
    zb = jnp.dot(u, wb_ref[...], preferred_element_type=F32)
    o_ref[0] = (zb * conv).astype(BF16)
    carry_ref[c] = v_ref[tm:tm + sub, :]

    rn = xn_ref.shape[1]
    rc = min(ROW_CHUNK, rn)
    for r in range(0, rn, rc):
        rows = pl.ds(pl.multiple_of(c * rn + r, rc), rc)
        u_ref[(tile + 1) % 2, rows, :] = _rms(xn_ref[0, r:r + rc, :], g_ref[...]).astype(BF16)

    @pl.when(t == pl.num_programs(1) - 1)
    def _():
        st_ref[0, c] = v_ref[sub + tm - CONV_STATE_LEN:sub + tm, :]


def _conv_in_prompt(x, g, w_in, layer, cw):
    b, s, d = x.shape
    tm, tn = min(TILES["conv_tm"], s), min(TILES["conv_tn"], d)
    pc = min(TILES["load_pc"], tm)
    nc = d // tn
    nt = s // tm
    rn = tm // nc
    assert rn % BF16_SUBLANES == 0, (tm, nc)

    def next_slab(i, t, c):
        nxt = jnp.minimum(i * nt + t + 1, b * nt - 1)
        return (nxt // nt, (nxt % nt) * nc + c, 0)

    gated, st = pl.pallas_call(
        functools.partial(_conv_in_prompt_kernel, tm=tm, pc=pc),
        grid=(b, nt, nc),
        in_specs=[
            pl.BlockSpec(memory_space=pl.ANY),
            pl.BlockSpec((1, rn, d), next_slab),
            pl.BlockSpec((1, d), lambda i, t, c: (0, 0)),
            pl.BlockSpec((d, tn), lambda i, t, c: (0, c)),
            pl.BlockSpec((d, tn), lambda i, t, c: (0, nc + c)),
            pl.BlockSpec((d, tn), lambda i, t, c: (0, 2 * nc + c)),
            pl.BlockSpec((None, CONV_WIDTH, tn), lambda i, t, c: (layer, 0, c)),
        ],
        out_specs=[
            pl.BlockSpec((1, tm, tn), lambda i, t, c: (i, t, c)),
            pl.BlockSpec((1, nc, CONV_STATE_LEN, tn), lambda i, t, c: (i, 0, 0, 0)),
        ],
        out_shape=[
            jax.ShapeDtypeStruct((b, s, d), BF16),
            jax.ShapeDtypeStruct((b, nc, CONV_STATE_LEN, tn), F32),
        ],
        scratch_shapes=[
            pltpu.VMEM((2, tm, d), BF16),
            pltpu.VMEM((F32_SUBLANES + tm, tn), F32),
            pltpu.VMEM((nc, F32_SUBLANES, tn), F32),
            pltpu.VMEM((LOAD_SLOTS, pc, d), F32),
            pltpu.SemaphoreType.DMA((LOAD_SLOTS,)),
        ],
        compiler_params=_params(("arbitrary", "arbitrary", "arbitrary")),
        name="conv_in_prompt",
    )(x, x, g, w_in, w_in, w_in, cw)
    return gated, st.transpose(0, 2, 1, 3).reshape(b, CONV_STATE_LEN, d)


def _conv_in_sample_kernel(x_ref, s0_ref, s1_ref, g_ref, wb_ref, wc_ref, wh_ref, cw_ref,
                           o_ref, st0_ref, st1_ref, u_ref, *, b, n_t):
    c = pl.program_id(0)

    @pl.when(c == 0)
    def _():
        for t in range(n_t):
            u_ref[t * b:(t + 1) * b, :] = _rms(x_ref[t], g_ref[...]).astype(BF16)

    u = u_ref[...]
    zc = jnp.dot(u, wc_ref[...], preferred_element_type=F32)
    zh = jnp.dot(u, wh_ref[...], preferred_element_type=F32)
    zb = jnp.dot(u, wb_ref[...], preferred_element_type=F32)
    v = zc * zh
    rows = [s0_ref[...], s1_ref[...]] + [v[t * b:(t + 1) * b] for t in range(n_t)]
    for t in range(n_t):
        conv = cw_ref[0:1, :] * rows[t]
        for k in range(1, CONV_WIDTH):
            conv = conv + cw_ref[k:k + 1, :] * rows[t + k]
        o_ref[t * b:(t + 1) * b, :] = (zb[t * b:(t + 1) * b] * conv).astype(BF16)
    st0_ref[...] = rows[n_t]
    st1_ref[...] = rows[n_t + 1]


def _conv_in_sample(x, state, g, w_in, layer, cw):
    n_t, b, d = x.shape
    tn = min(TILES["conv_tn"], d)
    nc = d // tn
    assert CONV_STATE_LEN == 2
    return pl.pallas_call(
        functools.partial(_conv_in_sample_kernel, b=b, n_t=n_t),
        grid=(nc,),
        in_specs=[
            pl.BlockSpec((n_t, b, d), lambda c: (0, 0, 0)),
            pl.BlockSpec((b, tn), lambda c: (0, c)),
            pl.BlockSpec((b, tn), lambda c: (0, nc + c)),
            pl.BlockSpec((1, d), lambda c: (0, 0)),
            pl.BlockSpec((d, tn), lambda c: (0, c)),
            pl.BlockSpec((d, tn), lambda c: (0, nc + c)),
            pl.BlockSpec((d, tn), lambda c: (0, 2 * nc + c)),
            pl.BlockSpec((None, CONV_WIDTH, tn), lambda c: (layer, 0, c)),
        ],
        out_specs=[
            pl.BlockSpec((n_t * b, tn), lambda c: (0, c)),
            pl.BlockSpec((b, tn), lambda c: (0, c)),
            pl.BlockSpec((b, tn), lambda c: (0, c)),
        ],
        out_shape=[
            jax.ShapeDtypeStruct((n_t * b, d), BF16),
            jax.ShapeDtypeStruct((b, d), F32),
            jax.ShapeDtypeStruct((b, d), F32),
        ],
        scratch_shapes=[pltpu.VMEM((n_t * b, d), BF16)],
        compiler_params=_params(("arbitrary",)),
        name="conv_in_sample",
    )(x, state, state, g, w_in, w_in, w_in, cw)


def _proj_res_kernel(a_ref, w_ref, x_ref, o_ref):
    o_ref[...] = x_ref[...] + jnp.dot(a_ref[...], w_ref[...], preferred_element_type=F32)


PROJ_INPUT_BUFFERS = 3


def _conv_out(a, w, x):
    t, d = x.shape
    tm, tn = min(TILES["proj_tm"], t), min(TILES["proj_tn"], d)
    deep = pl.Buffered(PROJ_INPUT_BUFFERS)

    def body(a_hbm, w_hbm, x_hbm, o_hbm):
        pltpu.emit_pipeline(
            _proj_res_kernel,
            grid=(t // tm, d // tn),
            in_specs=[
                pl.BlockSpec((tm, d), lambda i, n: (i, 0)),
                pl.BlockSpec((d, tn), lambda i, n: (0, n), pipeline_mode=deep),
                pl.BlockSpec((tm, tn), lambda i, n: (i, n), pipeline_mode=deep),
            ],
            out_specs=[pl.BlockSpec((tm, tn), lambda i, n: (i, n))],
        )(a_hbm, w_hbm, x_hbm, o_hbm)

    return pl.pallas_call(
        body,
        in_specs=[pl.BlockSpec(memory_space=pl.ANY)] * 3,
        out_specs=pl.BlockSpec(memory_space=pl.ANY),
        out_shape=jax.ShapeDtypeStruct((t, d), F32),
        compiler_params=pltpu.CompilerParams(vmem_limit_bytes=VMEM_LIMIT),
        name="conv_out",
    )(a, w, x)


def kernel(x_prompt, x_sample, state_pool, state_conv, norm_mix, norm_mlp, norm_final,
           w_pool, pool_scale, w_conv_in, conv_w, w_conv_out, w_up, w_down):
    bp, sp, d = x_prompt.shape
    bs, ss, _ = x_sample.shape

    w_pool_b = w_pool.astype(BF16)
    wup_b, wdn_b = (w_up, 0), (w_down, 0)
    wci_b = wco_b = None
    g_fin = norm_final.reshape(1, d)

    xp = x_prompt
    xs = x_sample.transpose(1, 0, 2)
    pool_p, conv_p, conv_s = [], [], []
    pool_s = None
    state_pool_t = state_pool.transpose(0, 2, 1, 3)
    for i in range(DEPTH):
        j = i // 2
        last = i == DEPTH - 1
        g_mix = norm_mix[i].reshape(1, d)
        if i % 2 == 0:
            sc = pool_scale[j].reshape(1, d)
            xp, st = _pool_prompt(xp, g_mix, w_pool_b, j, sc)
            pool_p.append(st)
            xs, pool_s = _pool_sample(xs, state_pool_t, g_mix, w_pool_b, j, sc, pool_s)
        else:
            a, st = _conv_in_prompt(xp, g_mix, wci_b, j, conv_w)
            conv_p.append(st)
            xp = _conv_out(a.reshape(bp * sp, d), wco_b, xp.reshape(bp * sp, d)).reshape(bp, sp, d)
            a, st0, st1 = _conv_in_sample(xs, state_conv[j].reshape(bs, CONV_STATE_LEN * d), g_mix,
                                          wci_b, j, conv_w)
            conv_s.append(jnp.stack([st0, st1], axis=1))
            xs = _conv_out(a, wco_b, xs.reshape(ss * bs, d)).reshape(ss, bs, d)
        g_mlp = norm_mlp[i].reshape(1, d)
        jobs = []
        if not last:
            if (i + 1) % 2 == 1:
                jobs += [(w_conv_in, (i + 1) // 2), (w_conv_out, (i + 1) // 2)]
            jobs += [(w_up, i + 1), (w_down, i + 1)]
        xs, emitted = _ffn(xs.reshape(ss * bs, d), g_mlp, g_fin, wup_b, wdn_b, final=last)
        xs = xs.reshape(ss, bs, d)
        if emitted:
            wup_b, wdn_b = emitted
        xp, cast = _ffn(xp.reshape(bp * sp, d), g_mlp, g_fin, wup_b, wdn_b, jobs, final=last)
        xp = xp.reshape(bp, sp, d)
        if not last:
            if (i + 1) % 2 == 1:
                wci_b, wco_b = cast[0], cast[1]
            wup_b, wdn_b = cast[-2], cast[-1]

    y_sample = xs.transpose(1, 0, 2)
    return (xp, y_sample, jnp.stack(pool_p), pool_s.transpose(0, 2, 1, 3), jnp.stack(conv_p), jnp.stack(conv_s))
```
